```python
import jax, jax.numpy as jnp
from jax import lax
import numpy as np

D_MODEL = 1024
BATCH = 32
SEQ = 2048
DEPTH = 1

CHUNK = 64
N_PREV_CHUNKS = 8
BAND = (N_PREV_CHUNKS + 1) * CHUNK

ATT_HEADS = 8
HEAD_DIM = 64
ATT_WIDTH = ATT_HEADS * HEAD_DIM
MAX_REL = 256

GMLP_GROUPS = 8
GMLP_GROUP_DIM = 64
GMLP_WIDTH = GMLP_GROUPS * GMLP_GROUP_DIM
GMLP_CHUNK = 128

MIX_WIDTH = GMLP_WIDTH + ATT_WIDTH
IN_WIDTH = 2 * GMLP_WIDTH + 3 * ATT_WIDTH
D_FF = 4 * D_MODEL

DEEPNORM_ALPHA = (2.0 * DEPTH) ** 0.25
DEEPNORM_BETA = (8.0 * DEPTH) ** -0.25
LN_EPS = 1e-5
RMS_EPS = 1e-6

kernel_name = "hybrid_gmlp_chunkattn_deepnorm"


def _layer_norm(x, g, b):
    xf = x.astype(jnp.float32)
    mu = jnp.mean(xf, axis=-1, keepdims=True)
    var = jnp.mean(jnp.square(xf - mu), axis=-1, keepdims=True)
    return ((xf - mu) * lax.rsqrt(var + LN_EPS) * g + b).astype(x.dtype)


def _rms_norm(x, g):
    xf = x.astype(jnp.float32)
    ms = jnp.mean(jnp.square(xf), axis=-1, keepdims=True)
    return (xf * lax.rsqrt(ms + RMS_EPS) * g).astype(x.dtype)


def _gmlp_spatial_gate(u, v, ln_g, ln_b, w_s, b_s):
    B, S, _ = u.shape
    nw = S // GMLP_CHUNK
    v = v.reshape(B, nw, GMLP_CHUNK, GMLP_GROUPS, GMLP_GROUP_DIM)
    vn = _layer_norm(v, ln_g, ln_b)
    pos = jnp.arange(GMLP_CHUNK)
    mask = (pos[:, None] // CHUNK) >= (pos[None, :] // CHUNK)
    w = jnp.where(mask[None], w_s, jnp.zeros_like(w_s))
    vm = jnp.einsum('gij,bwjgc->bwigc', w, vn) + jnp.transpose(b_s)[None, None, :, :, None]
    return u * vm.reshape(B, S, GMLP_WIDTH)


def _rel_bias(rel_table):
    i = jnp.arange(CHUNK)[:, None]
    m = jnp.arange(BAND)[None, :]
    dist = i + N_PREV_CHUNKS * CHUNK - m
    idx = jnp.clip(dist, -MAX_REL, MAX_REL) + MAX_REL
    return rel_table[:, idx]


def _chunked_band_attention(q, k, v, rel_table):
    B, S, _ = q.shape
    nc = S // CHUNK
    pad = N_PREV_CHUNKS * CHUNK
    qc_all = jnp.transpose(q.reshape(B, nc, CHUNK, ATT_HEADS, HEAD_DIM), (1, 0, 2, 3, 4))
    kp = jnp.pad(k.reshape(B, S, ATT_HEADS, HEAD_DIM), ((0, 0), (pad, 0), (0, 0), (0, 0)))
    vp = jnp.pad(v.reshape(B, S, ATT_HEADS, HEAD_DIM), ((0, 0), (pad, 0), (0, 0), (0, 0)))
    bias = _rel_bias(rel_table).astype(jnp.float32)[None]
    scale = HEAD_DIM ** -0.5
    key_offset = jnp.arange(BAND) - pad

    def one_chunk(args):
        qc, c = args
        kc = lax.dynamic_slice_in_dim(kp, c * CHUNK, BAND, axis=1)
        vc = lax.dynamic_slice_in_dim(vp, c * CHUNK, BAND, axis=1)
        s = jnp.einsum('bihd,bjhd->bhij', qc, kc,
                       preferred_element_type=jnp.float32) * scale + bias
        valid = (c * CHUNK + key_offset) >= 0
        s = jnp.where(valid[None, None, None, :], s, -jnp.inf)
        p = jax.nn.softmax(s, axis=-1).astype(vc.dtype)
        return jnp.einsum('bhij,bjhd->bihd', p, vc)

    out = lax.map(one_chunk, (qc_all, jnp.arange(nc)))
    return jnp.transpose(out, (1, 0, 2, 3, 4)).reshape(B, S, ATT_WIDTH)


def setup_inputs(seed: int = 0) -> dict:
    key = jax.random.key(seed)
    ks = jax.random.split(key, 20)
    f32 = jnp.float32
    x = jax.random.normal(ks[0], (BATCH, SEQ, D_MODEL), f32)

    col_scale = jnp.concatenate([
        jnp.full((2 * GMLP_WIDTH,), DEEPNORM_BETA, f32),
        jnp.ones((2 * ATT_WIDTH,), f32),
        jnp.full((ATT_WIDTH,), DEEPNORM_BETA, f32)])
    w_in = jax.random.normal(ks[1], (DEPTH, D_MODEL, IN_WIDTH), f32) * (D_MODEL ** -0.5) * col_scale

    gmlp_ln_g = 1.0 + 0.02 * jax.random.normal(ks[2], (DEPTH, GMLP_GROUPS, GMLP_GROUP_DIM), f32)
    gmlp_ln_b = 0.02 * jax.random.normal(ks[3], (DEPTH, GMLP_GROUPS, GMLP_GROUP_DIM), f32)
    w_spatial = 0.5 * (GMLP_CHUNK ** -0.5) * jax.random.normal(
        ks[4], (DEPTH, GMLP_GROUPS, GMLP_CHUNK, GMLP_CHUNK), f32)
    b_spatial = 1.0 + 0.02 * jax.random.normal(ks[5], (DEPTH, GMLP_GROUPS, GMLP_CHUNK), f32)

    rel_bias = 0.5 * jax.random.normal(ks[6], (DEPTH, ATT_HEADS, 2 * MAX_REL + 1), f32)

    gate_g_gmlp = 1.0 + 0.02 * jax.random.normal(ks[7], (DEPTH, GMLP_WIDTH), f32)
    gate_g_attn = 1.0 + 0.02 * jax.random.normal(ks[8], (DEPTH, ATT_WIDTH), f32)
    w_out = jax.random.normal(ks[9], (DEPTH, MIX_WIDTH, D_MODEL), f32) * (MIX_WIDTH ** -0.5) * DEEPNORM_BETA
    ln1_g = 1.0 + 0.02 * jax.random.normal(ks[10], (DEPTH, D_MODEL), f32)
    ln1_b = 0.02 * jax.random.normal(ks[11], (DEPTH, D_MODEL), f32)

    w_ff1 = jax.random.normal(ks[12], (DEPTH, D_MODEL, D_FF), f32) * (D_MODEL ** -0.5) * DEEPNORM_BETA
    w_ff2 = jax.random.normal(ks[13], (DEPTH, D_FF, D_MODEL), f32) * (D_FF ** -0.5) * DEEPNORM_BETA
    ln2_g = 1.0 + 0.02 * jax.random.normal(ks[14], (DEPTH, D_MODEL), f32)
    ln2_b = 0.02 * jax.random.normal(ks[15], (DEPTH, D_MODEL), f32)
    return {"x": x, "w_in": w_in, "gmlp_ln_g": gmlp_ln_g, "gmlp_ln_b": gmlp_ln_b,
            "w_spatial": w_spatial, "b_spatial": b_spatial, "rel_bias": rel_bias,
            "gate_g_gmlp": gate_g_gmlp, "gate_g_attn": gate_g_attn, "w_out": w_out,
            "ln1_g": ln1_g, "ln1_b": ln1_b, "w_ff1": w_ff1, "w_ff2": w_ff2,
            "ln2_g": ln2_g, "ln2_b": ln2_b}


def reference(x, w_in, gmlp_ln_g, gmlp_ln_b, w_spatial, b_spatial, rel_bias,
              gate_g_gmlp, gate_g_attn, w_out, ln1_g, ln1_b, w_ff1, w_ff2, ln2_g, ln2_b):
    h = x
    splits = [GMLP_WIDTH, 2 * GMLP_WIDTH, 2 * GMLP_WIDTH + ATT_WIDTH, 2 * GMLP_WIDTH + 2 * ATT_WIDTH]
    for l in range(DEPTH):
        z = jnp.einsum('bsd,de->bse', h, w_in[l])
        zu, zv, q, k, v = jnp.split(z, splits, axis=-1)
        y_a = _gmlp_spatial_gate(jax.nn.gelu(zu), jax.nn.gelu(zv),
                                 gmlp_ln_g[l], gmlp_ln_b[l], w_spatial[l], b_spatial[l])
        y_b = _chunked_band_attention(q, k, v, rel_bias[l])
        y = jnp.concatenate([_rms_norm(y_a, gate_g_gmlp[l]), _rms_norm(y_b, gate_g_attn[l])], axis=-1)
        h = _layer_norm(DEEPNORM_ALPHA * h + jnp.einsum('bse,ed->bsd', y, w_out[l]), ln1_g[l], ln1_b[l])
        f = jnp.einsum('bsf,fd->bsd', jnp.square(jax.nn.relu(jnp.einsum('bsd,df->bsf', h, w_ff1[l]))), w_ff2[l])
        h = _layer_norm(DEEPNORM_ALPHA * h + f, ln2_g[l], ln2_b[l])
    return h
```

```python
import functools

import jax
import jax.numpy as jnp
import numpy as np
from jax import lax
from jax.experimental import pallas as pl
from jax.experimental.pallas import tpu as pltpu

CHUNK = 64
N_PREV_CHUNKS = 8
BAND = (N_PREV_CHUNKS + 1) * CHUNK
PAD = N_PREV_CHUNKS * CHUNK
ATT_HEADS = 8
HEAD_DIM = 64
ATT_WIDTH = ATT_HEADS * HEAD_DIM
MAX_REL = 256
GMLP_GROUPS = 8
GMLP_GROUP_DIM = 64
GMLP_WIDTH = GMLP_GROUPS * GMLP_GROUP_DIM
GMLP_CHUNK = 128
LN_EPS = 1e-5
RMS_EPS = 1e-6

V7X_LANES = 128
V7X_VMEM_LIMIT_BYTES = 56 * 1024 * 1024

BF16 = jnp.bfloat16
F32 = jnp.float32


def _layer_norm_rows(t, g, b):
    mu = jnp.mean(t, axis=-1, keepdims=True)
    d = t - mu
    var = jnp.mean(d * d, axis=-1, keepdims=True)
    return d * lax.rsqrt(var + LN_EPS) * g + b


def _resident(shape):
    return pl.BlockSpec(shape, lambda *_: (0,) * len(shape), pipeline_mode=pl.Buffered(1))


def _in_proj_kernel(x_ref, w_ref, lng_ref, lnb_ref, avg_ref, u_ref, vn_ref, q_ref, k_ref, v_ref):
    xb = x_ref[...].astype(BF16)

    def proj(col):
        return jnp.dot(xb, w_ref[:, col * GMLP_WIDTH:(col + 1) * GMLP_WIDTH],
                       preferred_element_type=F32)

    u_ref[...] = jax.nn.gelu(proj(0)).astype(BF16)

    gv = jax.nn.gelu(proj(1))
    avg = avg_ref[...]
    half = avg.shape[0]

    def group_mean(t):
        tb = t.astype(BF16)
        return jnp.concatenate(
            [jnp.dot(tb[:, i * half:(i + 1) * half], avg, preferred_element_type=F32)
             for i in range(GMLP_WIDTH // half)], axis=-1)

    d = gv - group_mean(gv)
    var = group_mean(d * d)
    vn_ref[...] = (d * lax.rsqrt(var + LN_EPS) * lng_ref[...] + lnb_ref[...]).astype(BF16)

    q_ref[...] = (proj(2) * (HEAD_DIM ** -0.5)).astype(BF16)
    k_ref[...] = proj(3).astype(BF16)
    v_ref[...] = proj(4).astype(BF16)


def _in_proj(x2, w_in, ln_g, ln_b, avg, tm):
    n, d = x2.shape
    width = GMLP_WIDTH
    out = jax.ShapeDtypeStruct((n, width), BF16)
    tile = pl.BlockSpec((tm, width), lambda i: (i, 0))
    return pl.pallas_call(
        _in_proj_kernel,
        grid=(n // tm,),
        in_specs=[pl.BlockSpec((tm, d), lambda i: (i, 0)),
                  _resident(w_in.shape), _resident(ln_g.shape), _resident(ln_b.shape),
                  _resident(avg.shape)],
        out_specs=[tile] * 5,
        out_shape=[out] * 5,
        compiler_params=pltpu.CompilerParams(
            dimension_semantics=("parallel",), vmem_limit_bytes=V7X_VMEM_LIMIT_BYTES),
        name="in_proj",
    )(x2, w_in, ln_g, ln_b, avg)


def _mixer_kernel(u_ref, vn_ref, q_ref, k_ref, v_ref, ws_ref, bs_ref, bias_ref, ga_ref, gb_ref,
                  y_ref, kpad_ref, vpad_ref):
    seq = u_ref.shape[0]
    n_pairs = ATT_WIDTH // V7X_LANES

    row = lax.broadcasted_iota(jnp.int32, (GMLP_CHUNK, GMLP_CHUNK), 0) // CHUNK
    col = lax.broadcasted_iota(jnp.int32, (GMLP_CHUNK, GMLP_CHUNK), 1) // CHUNK
    causal = jnp.concatenate([row >= col, row >= col], axis=0)
    first_group = lax.broadcasted_iota(jnp.int32, (GMLP_CHUNK, V7X_LANES), 1) < GMLP_GROUP_DIM
    ws = [jnp.where(causal, ws_ref[p], jnp.zeros_like(ws_ref[p])) for p in range(n_pairs)]

    def window(w, carry):
        rows = pl.ds(pl.multiple_of(w * GMLP_CHUNK, GMLP_CHUNK), GMLP_CHUNK)
        parts = []
        for p in range(n_pairs):
            lanes = slice(p * V7X_LANES, (p + 1) * V7X_LANES)
            r = jnp.dot(ws[p], vn_ref[rows, lanes], preferred_element_type=F32)
            parts.append(jnp.where(first_group, r[:GMLP_CHUNK], r[GMLP_CHUNK:]))
        vm = jnp.concatenate(parts, axis=-1) + bs_ref[...]
        ya = u_ref[rows, :].astype(F32) * vm
        ms = jnp.mean(ya * ya, axis=-1, keepdims=True)
        y_ref[rows, :GMLP_WIDTH] = (ya * lax.rsqrt(ms + RMS_EPS) * ga_ref[...]).astype(BF16)
        return carry

    lax.fori_loop(0, seq // GMLP_CHUNK, window, 0)

    kpad_ref[:PAD, :] = jnp.zeros((PAD, ATT_WIDTH), BF16)
    vpad_ref[:PAD, :] = jnp.zeros((PAD, ATT_WIDTH), BF16)
    kpad_ref[PAD:, :] = k_ref[...]
    vpad_ref[PAD:, :] = v_ref[...]

    key_pos = lax.broadcasted_iota(jnp.int32, (1, BAND), 1) - PAD
    lane = lax.broadcasted_iota(jnp.int32, (CHUNK, V7X_LANES), 1)
    head_lanes = [lane < HEAD_DIM, lane >= HEAD_DIM]

    def chunk(c, carry):
        start = pl.multiple_of(c * CHUNK, CHUNK)
        qrows = pl.ds(start, CHUNK)
        band = pl.ds(start, BAND)
        valid = (start + key_pos) >= 0
        parts = []
        for p in range(n_pairs):
            lanes = slice(p * V7X_LANES, (p + 1) * V7X_LANES)
            qp = q_ref[qrows, lanes]
            kp = kpad_ref[band, lanes]
            vp = vpad_ref[band, lanes]
            outs = []
            for j in range(2):
                qh = jnp.where(head_lanes[j], qp, jnp.zeros_like(qp))
                s = lax.dot_general(qh, kp, (((1,), (1,)), ((), ())),
                                    preferred_element_type=F32)
                s = jnp.where(valid, s + bias_ref[2 * p + j], -jnp.inf)
                e = jnp.exp(s - jnp.max(s, axis=-1, keepdims=True))
                denom = jnp.sum(e, axis=-1, keepdims=True)
                o = jnp.dot(e.astype(BF16), vp, preferred_element_type=F32)
                outs.append(o / denom)
            parts.append(jnp.where(head_lanes[0], outs[0], outs[1]))
        yb = jnp.concatenate(parts, axis=-1)
        ms = jnp.mean(yb * yb, axis=-1, keepdims=True)
        y_ref[qrows, GMLP_WIDTH:] = (yb * lax.rsqrt(ms + RMS_EPS) * gb_ref[...]).astype(BF16)
        return carry

    lax.fori_loop(0, seq // CHUNK, chunk, 0)


def _mixer(u, vn, q, k, v, ws, bs, bias, ga, gb, seq):
    n = u.shape[0]
    row = pl.BlockSpec((seq, GMLP_WIDTH), lambda b: (b, 0))
    return pl.pallas_call(
        _mixer_kernel,
        grid=(n // seq,),
        in_specs=[row] * 5 + [_resident(ws.shape), _resident(bs.shape), _resident(bias.shape),
                              _resident(ga.shape), _resident(gb.shape)],
        out_specs=pl.BlockSpec((seq, GMLP_WIDTH + ATT_WIDTH), lambda b: (b, 0)),
        out_shape=jax.ShapeDtypeStruct((n, GMLP_WIDTH + ATT_WIDTH), BF16),
        scratch_shapes=[pltpu.VMEM((PAD + seq, ATT_WIDTH), BF16),
                        pltpu.VMEM((PAD + seq, ATT_WIDTH), BF16)],
        compiler_params=pltpu.CompilerParams(
            dimension_semantics=("parallel",), vmem_limit_bytes=V7X_VMEM_LIMIT_BYTES),
        name="mixer",
    )(u, vn, q, k, v, ws, bs, bias, ga, gb)


def _ffn_kernel(alpha, ff_chunk, y_ref, x_ref, wout_ref, w1_ref, w2_ref,
                g1_ref, b1_ref, g2_ref, b2_ref, o_ref):
    t = jnp.dot(y_ref[...], wout_ref[...], preferred_element_type=F32)
    h = _layer_norm_rows(alpha * x_ref[...] + t, g1_ref[...], b1_ref[...])
    hb = h.astype(BF16)
    acc = alpha * h
    for c in range(w1_ref.shape[1] // ff_chunk):
        cols = slice(c * ff_chunk, (c + 1) * ff_chunk)
        a = jnp.maximum(jnp.dot(hb, w1_ref[:, cols], preferred_element_type=F32), 0.0)
        acc = acc + jnp.dot((a * a).astype(BF16), w2_ref[cols, :], preferred_element_type=F32)
    o_ref[...] = _layer_norm_rows(acc, g2_ref[...], b2_ref[...])


def _ffn(y, x2, w_out, w1, w2, g1, b1, g2, b2, alpha, tm, ff_chunk):
    n, d = x2.shape
    tile = pl.BlockSpec((tm, d), lambda i: (i, 0))
    return pl.pallas_call(
        functools.partial(_ffn_kernel, alpha, ff_chunk),
        grid=(n // tm,),
        in_specs=[tile, tile, _resident(w_out.shape), _resident(w1.shape), _resident(w2.shape),
                  _resident(g1.shape), _resident(b1.shape), _resident(g2.shape),
                  _resident(b2.shape)],
        out_specs=tile,
        out_shape=jax.ShapeDtypeStruct((n, d), F32),
        compiler_params=pltpu.CompilerParams(
            dimension_semantics=("parallel",), vmem_limit_bytes=V7X_VMEM_LIMIT_BYTES),
        name="ffn",
    )(y, x2, w_out, w1, w2, g1, b1, g2, b2)


def _rel_bias_table(rel_table):
    i = np.arange(CHUNK)[:, None]
    m = np.arange(BAND)[None, :]
    idx = np.clip(i + PAD - m, -MAX_REL, MAX_REL) + MAX_REL
    return rel_table[:, idx].astype(F32)


def kernel(x, w_in, gmlp_ln_g, gmlp_ln_b, w_spatial, b_spatial, rel_bias,
           gate_g_gmlp, gate_g_attn, w_out, ln1_g, ln1_b, w_ff1, w_ff2, ln2_g, ln2_b):
    batch, seq, d_model = x.shape
    depth = w_in.shape[0]
    alpha = (2.0 * depth) ** 0.25
    tm = 512

    avg_half = 2 * V7X_LANES
    group_of = np.arange(avg_half) // GMLP_GROUP_DIM
    avg = jnp.asarray((group_of[:, None] == group_of[None, :]) / GMLP_GROUP_DIM, BF16)

    h = x.reshape(batch * seq, d_model)
    for l in range(depth):
        u, vn, q, k, v = _in_proj(
            h, w_in[l].astype(BF16),
            gmlp_ln_g[l].reshape(1, GMLP_WIDTH), gmlp_ln_b[l].reshape(1, GMLP_WIDTH), avg, tm)
        ws = w_spatial[l].astype(BF16).reshape(GMLP_GROUPS // 2, 2 * GMLP_CHUNK, GMLP_CHUNK)
        bs = jnp.repeat(jnp.transpose(b_spatial[l]), GMLP_GROUP_DIM, axis=1)
        y = _mixer(u, vn, q, k, v, ws, bs, _rel_bias_table(rel_bias[l]),
                   gate_g_gmlp[l].reshape(1, GMLP_WIDTH), gate_g_attn[l].reshape(1, ATT_WIDTH), seq)
        h = _ffn(y, h, w_out[l].astype(BF16), w_ff1[l].astype(BF16), w_ff2[l].astype(BF16),
                 ln1_g[l].reshape(1, d_model), ln1_b[l].reshape(1, d_model),
                 ln2_g[l].reshape(1, d_model), ln2_b[l].reshape(1, d_model), alpha, tm, 1024)
    return h.reshape(batch, seq, d_model)
```

```python
import functools

import jax
import jax.numpy as jnp
import numpy as np
from jax import lax
from jax.experimental import pallas as pl
from jax.experimental.pallas import tpu as pltpu

CHUNK = 64
N_PREV_CHUNKS = 8
BAND = (N_PREV_CHUNKS + 1) * CHUNK
PAD = N_PREV_CHUNKS * CHUNK
ATT_HEADS = 8
HEAD_DIM = 64
ATT_WIDTH = ATT_HEADS * HEAD_DIM
MAX_REL = 256
GMLP_GROUPS = 8
GMLP_GROUP_DIM = 64
GMLP_WIDTH = GMLP_GROUPS * GMLP_GROUP_DIM
GMLP_CHUNK = 128
LN_EPS = 1e-5
RMS_EPS = 1e-6
LOG2E = 1.4426950408889634

Q_BLOCK = 256
KEY_TILE = 256
BAND_BLOCK = PAD + Q_BLOCK

V7X_LANES = 128
V7X_VMEM_LIMIT_BYTES = 56 * 1024 * 1024

BF16 = jnp.bfloat16
F32 = jnp.float32


def _layer_norm_rows(t, g, b):
    mu = jnp.mean(t, axis=-1, keepdims=True)
    d = t - mu
    var = jnp.mean(d * d, axis=-1, keepdims=True)
    return d * lax.rsqrt(var + LN_EPS) * g + b


def _resident(shape):
    return pl.BlockSpec(shape, lambda *_: (0,) * len(shape), pipeline_mode=pl.Buffered(1))


def _in_proj_kernel(x_ref, w_ref, lng_ref, lnb_ref, avg_ref, u_ref, vn_ref, q_ref, k_ref, v_ref):
    xb = x_ref[...].astype(BF16)

    def proj(col):
        return jnp.dot(xb, w_ref[:, col * GMLP_WIDTH:(col + 1) * GMLP_WIDTH],
                       preferred_element_type=F32)

    u_ref[...] = jax.nn.gelu(proj(0)).astype(BF16)

    gv = jax.nn.gelu(proj(1))
    avg = avg_ref[...]
    half = avg.shape[0]

    def group_mean(t):
        tb = t.astype(BF16)
        return jnp.concatenate(
            [jnp.dot(tb[:, i * half:(i + 1) * half], avg, preferred_element_type=F32)
             for i in range(GMLP_WIDTH // half)], axis=-1)

    d = gv - group_mean(gv)
    var = group_mean(d * d)
    vn_ref[...] = (d * lax.rsqrt(var + LN_EPS) * lng_ref[...] + lnb_ref[...]).astype(BF16)

    q_ref[...] = (proj(2) * (HEAD_DIM ** -0.5 * LOG2E)).astype(BF16)
    k_ref[...] = proj(3).astype(BF16)
    v_ref[...] = proj(4).astype(BF16)


def _in_proj(x2, w_in, ln_g, ln_b, avg, tm):
    n, d = x2.shape
    width = GMLP_WIDTH
    out = jax.ShapeDtypeStruct((n, width), BF16)
    tile = pl.BlockSpec((tm, width), lambda i: (i, 0))
    return pl.pallas_call(
        _in_proj_kernel,
        grid=(n // tm,),
        in_specs=[pl.BlockSpec((tm, d), lambda i: (i, 0)),
                  _resident(w_in.shape), _resident(ln_g.shape), _resident(ln_b.shape),
                  _resident(avg.shape)],
        out_specs=[tile] * 5,
        out_shape=[out] * 5,
        compiler_params=pltpu.CompilerParams(
            dimension_semantics=("parallel",), vmem_limit_bytes=V7X_VMEM_LIMIT_BYTES),
        name="in_proj",
    )(x2, w_in, ln_g, ln_b, avg)


def _mixer_kernel(u_ref, vn_ref, q_ref, k_ref, v_ref, ws_ref, bs_ref, bias_ref, ga_ref, gb_ref,
                  y_ref, vx_ref, s_ref, m_ref, p_ref, o_ref):
    seq = u_ref.shape[0]
    n_pairs = ATT_WIDTH // V7X_LANES

    row = lax.broadcasted_iota(jnp.int32, (GMLP_CHUNK, GMLP_CHUNK), 0) // CHUNK
    col = lax.broadcasted_iota(jnp.int32, (GMLP_CHUNK, GMLP_CHUNK), 1) // CHUNK
    causal = jnp.concatenate([row >= col, row >= col], axis=0)
    first_group = lax.broadcasted_iota(jnp.int32, (GMLP_CHUNK, V7X_LANES), 1) < GMLP_GROUP_DIM
    ws = [jnp.where(causal, ws_ref[p], jnp.zeros_like(ws_ref[p])) for p in range(n_pairs)]

    def window(w, carry):
        rows = pl.ds(pl.multiple_of(w * GMLP_CHUNK, GMLP_CHUNK), GMLP_CHUNK)
        parts = []
        for p in range(n_pairs):
            lanes = slice(p * V7X_LANES, (p + 1) * V7X_LANES)
            r = jnp.dot(ws[p], vn_ref[rows, lanes], preferred_element_type=F32)
            parts.append(jnp.where(first_group, r[:GMLP_CHUNK], r[GMLP_CHUNK:]))
        vm = jnp.concatenate(parts, axis=-1) + bs_ref[...]
        ya = u_ref[rows, :].astype(F32) * vm
        ms = jnp.mean(ya * ya, axis=-1, keepdims=True)
        y_ref[rows, :GMLP_WIDTH] = (ya * lax.rsqrt(ms + RMS_EPS) * ga_ref[...]).astype(BF16)
        return carry

    lax.fori_loop(0, seq // GMLP_CHUNK, window, 0)

    lane_full = lax.broadcasted_iota(jnp.int32, (seq, V7X_LANES), 1)
    for h in range(ATT_HEADS):
        pair = slice((h // 2) * V7X_LANES, (h // 2 + 1) * V7X_LANES)
        own = (lane_full < HEAD_DIM) if h % 2 == 0 else (lane_full >= HEAD_DIM)
        vx_ref[:, h * V7X_LANES:(h + 1) * V7X_LANES] = jnp.where(
            own, v_ref[:, pair], jnp.ones((seq, V7X_LANES), BF16))

    lane = lax.broadcasted_iota(jnp.int32, (Q_BLOCK, V7X_LANES), 1)
    head_lanes = [lane < HEAD_DIM, lane >= HEAD_DIM]

    def attn_block(qrow0, tiles):
        nk = len(tiles) * KEY_TILE
        koff = tiles[0] * KEY_TILE
        qrows = pl.ds(qrow0, Q_BLOCK)
        krow0 = qrow0 - PAD + koff
        if not isinstance(krow0, int):
            krow0 = pl.multiple_of(krow0, KEY_TILE)
        krows = pl.ds(krow0, nk)

        def scores(h):
            pair = slice((h // 2) * V7X_LANES, (h // 2 + 1) * V7X_LANES)
            qp = q_ref[qrows, pair]
            qh = jnp.where(head_lanes[h % 2], qp, jnp.zeros_like(qp))
            s = lax.dot_general(qh, k_ref[krows, pair], (((1,), (1,)), ((), ())),
                                preferred_element_type=F32)
            s_ref[h % 2, :, :nk] = s + bias_ref[h, :, koff:koff + nk]

        def probs(h):
            m = jnp.max(s_ref[h % 2, :, :nk], axis=-1, keepdims=True)
            m_ref[...] = m
            p_ref[h % 2, :, :nk] = jnp.exp2(s_ref[h % 2, :, :nk] - m_ref[...]).astype(BF16)

        def weighted(h):
            o_ref[:, h * V7X_LANES:(h + 1) * V7X_LANES] = jnp.dot(
                p_ref[h % 2, :, :nk], vx_ref[krows, h * V7X_LANES:(h + 1) * V7X_LANES],
                preferred_element_type=F32)

        scores(0)
        for h in range(ATT_HEADS):
            if h + 1 < ATT_HEADS:
                scores(h + 1)
            probs(h)
            if h >= 1:
                weighted(h - 1)
        weighted(ATT_HEADS - 1)

        parts = []
        for p in range(n_pairs):
            halves = []
            for j in range(2):
                o = o_ref[:, (2 * p + j) * V7X_LANES:(2 * p + j + 1) * V7X_LANES]
                halves.append(o / pltpu.roll(o, HEAD_DIM, 1))
            parts.append(jnp.where(head_lanes[0], halves[0], halves[1]))
        yb = jnp.concatenate(parts, axis=-1)
        ms = jnp.mean(yb * yb, axis=-1, keepdims=True)
        y_ref[qrows, GMLP_WIDTH:] = (yb * lax.rsqrt(ms + RMS_EPS) * gb_ref[...]).astype(BF16)

    band_tiles = BAND_BLOCK // KEY_TILE
    first_full = PAD // Q_BLOCK
    for b in range(first_full):
        attn_block(b * Q_BLOCK, tuple(range(band_tiles - 1 - b, band_tiles)))

    def full_block(b, carry):
        attn_block(pl.multiple_of(b * Q_BLOCK, Q_BLOCK), tuple(range(band_tiles)))
        return carry

    lax.fori_loop(first_full, seq // Q_BLOCK, full_block, 0)


def _mixer(u, vn, q, k, v, ws, bs, bias, ga, gb, seq):
    n = u.shape[0]
    row = pl.BlockSpec((seq, GMLP_WIDTH), lambda b: (b, 0))
    return pl.pallas_call(
        _mixer_kernel,
        grid=(n // seq,),
        in_specs=[row] * 5 + [_resident(ws.shape), _resident(bs.shape), _resident(bias.shape),
                              _resident(ga.shape), _resident(gb.shape)],
        out_specs=pl.BlockSpec((seq, GMLP_WIDTH + ATT_WIDTH), lambda b: (b, 0)),
        out_shape=jax.ShapeDtypeStruct((n, GMLP_WIDTH + ATT_WIDTH), BF16),
        scratch_shapes=[pltpu.VMEM((seq, ATT_HEADS * V7X_LANES), BF16),
                        pltpu.VMEM((2, Q_BLOCK, BAND_BLOCK), F32),
                        pltpu.VMEM((Q_BLOCK, 1), F32),
                        pltpu.VMEM((2, Q_BLOCK, BAND_BLOCK), BF16),
                        pltpu.VMEM((Q_BLOCK, ATT_HEADS * V7X_LANES), F32)],
        compiler_params=pltpu.CompilerParams(
            dimension_semantics=("parallel",), vmem_limit_bytes=V7X_VMEM_LIMIT_BYTES),
        name="mixer",
    )(u, vn, q, k, v, ws, bs, bias, ga, gb)


def _ffn_kernel(alpha, ff_chunk, y_ref, x_ref, wout_ref, w1_ref, w2_ref,
                g1_ref, b1_ref, g2_ref, b2_ref, o_ref):
    t = jnp.dot(y_ref[...], wout_ref[...], preferred_element_type=F32)
    h = _layer_norm_rows(alpha * x_ref[...] + t, g1_ref[...], b1_ref[...])
    hb = h.astype(BF16)
    acc = alpha * h
    for c in range(w1_ref.shape[1] // ff_chunk):
        cols = slice(c * ff_chunk, (c + 1) * ff_chunk)
        a = jnp.maximum(jnp.dot(hb, w1_ref[:, cols], preferred_element_type=F32), 0.0)
        acc = acc + jnp.dot((a * a).astype(BF16), w2_ref[cols, :], preferred_element_type=F32)
    o_ref[...] = _layer_norm_rows(acc, g2_ref[...], b2_ref[...])


def _ffn(y, x2, w_out, w1, w2, g1, b1, g2, b2, alpha, tm, ff_chunk):
    n, d = x2.shape
    tile = pl.BlockSpec((tm, d), lambda i: (i, 0))
    return pl.pallas_call(
        functools.partial(_ffn_kernel, alpha, ff_chunk),
        grid=(n // tm,),
        in_specs=[tile, tile, _resident(w_out.shape), _resident(w1.shape), _resident(w2.shape),
                  _resident(g1.shape), _resident(b1.shape), _resident(g2.shape),
                  _resident(b2.shape)],
        out_specs=tile,
        out_shape=jax.ShapeDtypeStruct((n, d), F32),
        compiler_params=pltpu.CompilerParams(
            dimension_semantics=("parallel",), vmem_limit_bytes=V7X_VMEM_LIMIT_BYTES),
        name="ffn",
    )(y, x2, w_out, w1, w2, g1, b1, g2, b2)


def _rel_bias_blocks(rel_table):
    heads = rel_table.shape[0]
    n_diag = Q_BLOCK + BAND_BLOCK - 1
    dist = PAD + (Q_BLOCK - 1) - np.arange(n_diag)
    diag = rel_table[:, np.clip(dist, -MAX_REL, MAX_REL) + MAX_REL].astype(F32) * LOG2E
    ext = jnp.pad(diag, ((0, 0), (0, 1)))
    skew = jnp.tile(ext, (1, Q_BLOCK))[:, :Q_BLOCK * n_diag].reshape(heads, Q_BLOCK, n_diag)
    toeplitz = skew[:, :, Q_BLOCK - 1:Q_BLOCK - 1 + BAND_BLOCK]
    q_chunk = np.arange(Q_BLOCK)[:, None] // CHUNK
    k_chunk = np.arange(BAND_BLOCK)[None, :] // CHUNK
    in_band = (k_chunk >= q_chunk) & (k_chunk <= q_chunk + N_PREV_CHUNKS)
    return jnp.where(in_band[None], toeplitz, -jnp.inf)


def kernel(x, w_in, gmlp_ln_g, gmlp_ln_b, w_spatial, b_spatial, rel_bias,
           gate_g_gmlp, gate_g_attn, w_out, ln1_g, ln1_b, w_ff1, w_ff2, ln2_g, ln2_b):
    batch, seq, d_model = x.shape
    depth = w_in.shape[0]
    alpha = (2.0 * depth) ** 0.25
    tm = 512

    avg_half = 2 * V7X_LANES
    group_of = np.arange(avg_half) // GMLP_GROUP_DIM
    avg = jnp.asarray((group_of[:, None] == group_of[None, :]) / GMLP_GROUP_DIM, BF16)

    h = x.reshape(batch * seq, d_model)
    for l in range(depth):
        u, vn, q, k, v = _in_proj(
            h, w_in[l].astype(BF16),
            gmlp_ln_g[l].reshape(1, GMLP_WIDTH), gmlp_ln_b[l].reshape(1, GMLP_WIDTH), avg, tm)
        ws = w_spatial[l].astype(BF16).reshape(GMLP_GROUPS // 2, 2 * GMLP_CHUNK, GMLP_CHUNK)
        bs = jnp.repeat(jnp.transpose(b_spatial[l]), GMLP_GROUP_DIM, axis=1)
        y = _mixer(u, vn, q, k, v, ws, bs, _rel_bias_blocks(rel_bias[l]),
                   gate_g_gmlp[l].reshape(1, GMLP_WIDTH), gate_g_attn[l].reshape(1, ATT_WIDTH), seq)
        h = _ffn(y, h, w_out[l].astype(BF16), w_ff1[l].astype(BF16), w_ff2[l].astype(BF16),
                 ln1_g[l].reshape(1, d_model), ln1_b[l].reshape(1, d_model),
                 ln2_g[l].reshape(1, d_model), ln2_b[l].reshape(1, d_model), alpha, tm, 1024)
    return h.reshape(batch, seq, d_model)
```

```python
import functools

import jax
import jax.numpy as jnp
import numpy as np
from jax import lax
from jax.experimental import pallas as pl
from jax.experimental.pallas import tpu as pltpu

CHUNK = 64
N_PREV_CHUNKS = 8
BAND = (N_PREV_CHUNKS + 1) * CHUNK
PAD = N_PREV_CHUNKS * CHUNK
ATT_HEADS = 8
HEAD_DIM = 64
ATT_WIDTH = ATT_HEADS * HEAD_DIM
MAX_REL = 256
GMLP_GROUPS = 8
GMLP_GROUP_DIM = 64
GMLP_WIDTH = GMLP_GROUPS * GMLP_GROUP_DIM
GMLP_CHUNK = 128
LN_EPS = 1e-5
RMS_EPS = 1e-6
LOG2E = 1.4426950408889634

Q_BLOCK = 256
KEY_TILE = 256
BAND_BLOCK = PAD + Q_BLOCK

V7X_LANES = 128
V7X_VMEM_LIMIT_BYTES = 56 * 1024 * 1024

BF16 = jnp.bfloat16
F32 = jnp.float32


def _layer_norm_rows(t, g, b):
    mu = jnp.mean(t, axis=-1, keepdims=True)
    d = t - mu
    var = jnp.mean(d * d, axis=-1, keepdims=True)
    return d * lax.rsqrt(var + LN_EPS) * g + b


def _resident(shape):
    return pl.BlockSpec(shape, lambda *_: (0,) * len(shape), pipeline_mode=pl.Buffered(1))


def _in_proj_kernel(x_ref, w_ref, lng_ref, lnb_ref, avg_ref, u_ref, vn_ref, q_ref, k_ref, v_ref):
    xb = x_ref[...].astype(BF16)

    def proj(col):
        return jnp.dot(xb, w_ref[:, col * GMLP_WIDTH:(col + 1) * GMLP_WIDTH],
                       preferred_element_type=F32)

    u_ref[...] = jax.nn.gelu(proj(0)).astype(BF16)

    gv = jax.nn.gelu(proj(1))
    avg = avg_ref[...]
    half = avg.shape[0]

    def group_mean(t):
        tb = t.astype(BF16)
        return jnp.concatenate(
            [jnp.dot(tb[:, i * half:(i + 1) * half], avg, preferred_element_type=F32)
             for i in range(GMLP_WIDTH // half)], axis=-1)

    d = gv - group_mean(gv)
    var = group_mean(d * d)
    vn_ref[...] = (d * lax.rsqrt(var + LN_EPS) * lng_ref[...] + lnb_ref[...]).astype(BF16)

    q_ref[...] = (proj(2) * (HEAD_DIM ** -0.5 * LOG2E)).astype(BF16)
    k_ref[...] = proj(3).astype(BF16)
    v_ref[...] = proj(4).astype(BF16)


def _in_proj(x2, w_in, ln_g, ln_b, avg, tm):
    n, d = x2.shape
    width = GMLP_WIDTH
    out = jax.ShapeDtypeStruct((n, width), BF16)
    tile = pl.BlockSpec((tm, width), lambda i: (i, 0))
    return pl.pallas_call(
        _in_proj_kernel,
        grid=(n // tm,),
        in_specs=[pl.BlockSpec((tm, d), lambda i: (i, 0)),
                  _resident(w_in.shape), _resident(ln_g.shape), _resident(ln_b.shape),
                  _resident(avg.shape)],
        out_specs=[tile] * 5,
        out_shape=[out] * 5,
        compiler_params=pltpu.CompilerParams(
            dimension_semantics=("parallel",), vmem_limit_bytes=V7X_VMEM_LIMIT_BYTES),
        name="in_proj",
    )(x2, w_in, ln_g, ln_b, avg)


def _mixer_kernel(u_ref, vn_ref, q_ref, k_ref, v_ref, ws_ref, bs_ref, bias_ref, ga_ref, gb_ref,
                  y_ref, wsm_ref, vx_ref, s_ref, m_ref, p_ref, o_ref):
    seq = u_ref.shape[0]
    n_pairs = ATT_WIDTH // V7X_LANES

    row = lax.broadcasted_iota(jnp.int32, (GMLP_CHUNK, GMLP_CHUNK), 0) // CHUNK
    col = lax.broadcasted_iota(jnp.int32, (GMLP_CHUNK, GMLP_CHUNK), 1) // CHUNK
    causal = jnp.concatenate([row >= col, row >= col], axis=0)
    first_group = lax.broadcasted_iota(jnp.int32, (GMLP_CHUNK, V7X_LANES), 1) < GMLP_GROUP_DIM
    for p in range(n_pairs):
        wsm_ref[p] = jnp.where(causal, ws_ref[p], jnp.zeros_like(ws_ref[p]))

    def gate_windows(row0):
        win = [pl.ds(row0 + i * GMLP_CHUNK, GMLP_CHUNK) for i in range(2)]
        parts = [[], []]
        for p in range(n_pairs):
            lanes = slice(p * V7X_LANES, (p + 1) * V7X_LANES)
            vals = jnp.concatenate([vn_ref[win[0], lanes], vn_ref[win[1], lanes]], axis=1)
            r = jnp.dot(wsm_ref[p], vals, preferred_element_type=F32)
            for i in range(2):
                cols = slice(i * V7X_LANES, (i + 1) * V7X_LANES)
                parts[i].append(jnp.where(first_group, r[:GMLP_CHUNK, cols], r[GMLP_CHUNK:, cols]))
        for i in range(2):
            vm = jnp.concatenate(parts[i], axis=-1) + bs_ref[...]
            ya = u_ref[win[i], :].astype(F32) * vm
            ms = jnp.mean(ya * ya, axis=-1, keepdims=True)
            y_ref[win[i], :GMLP_WIDTH] = (ya * lax.rsqrt(ms + RMS_EPS) * ga_ref[...]).astype(BF16)

    lane_full = lax.broadcasted_iota(jnp.int32, (seq, V7X_LANES), 1)
    for h in range(ATT_HEADS):
        pair = slice((h // 2) * V7X_LANES, (h // 2 + 1) * V7X_LANES)
        own = (lane_full < HEAD_DIM) if h % 2 == 0 else (lane_full >= HEAD_DIM)
        vx_ref[:, h * V7X_LANES:(h + 1) * V7X_LANES] = jnp.where(
            own, v_ref[:, pair], jnp.zeros((seq, V7X_LANES), BF16))

    lane = lax.broadcasted_iota(jnp.int32, (Q_BLOCK, V7X_LANES), 1)
    head_lanes = [lane < HEAD_DIM, lane >= HEAD_DIM]

    def attn_block(qrow0, tiles):
        nk = len(tiles) * KEY_TILE
        koff = tiles[0] * KEY_TILE
        qrows = pl.ds(qrow0, Q_BLOCK)
        krow0 = qrow0 - PAD + koff
        if not isinstance(krow0, int):
            krow0 = pl.multiple_of(krow0, KEY_TILE)
        krows = pl.ds(krow0, nk)

        def scores(h):
            pair = slice((h // 2) * V7X_LANES, (h // 2 + 1) * V7X_LANES)
            qp = q_ref[qrows, pair]
            qh = jnp.where(head_lanes[h % 2], qp, jnp.zeros_like(qp))
            s = lax.dot_general(qh, k_ref[krows, pair], (((1,), (1,)), ((), ())),
                                preferred_element_type=F32)
            s_ref[h % 2, :, :nk] = s + bias_ref[h, :, koff:koff + nk]

        def probs(h):
            m = jnp.max(s_ref[h % 2, :, :nk], axis=-1, keepdims=True)
            m_ref[h % 2] = jnp.broadcast_to(m, (Q_BLOCK, V7X_LANES))
            for t in range(nk // V7X_LANES):
                cols = slice(t * V7X_LANES, (t + 1) * V7X_LANES)
                p_ref[h % 2, :, cols] = jnp.exp2(s_ref[h % 2, :, cols] - m_ref[h % 2]).astype(BF16)

        def weighted(h):
            vals = jnp.concatenate([vx_ref[krows, h * V7X_LANES:(h + 1) * V7X_LANES],
                                    jnp.ones((nk, V7X_LANES), BF16)], axis=1)
            o = jnp.dot(p_ref[h % 2, :, :nk], vals, preferred_element_type=F32)
            o_ref[:, h * V7X_LANES:(h + 1) * V7X_LANES] = o[:, :V7X_LANES] / o[:, V7X_LANES:]

        gate_windows(qrow0)

        scores(0)
        for h in range(ATT_HEADS):
            if h + 1 < ATT_HEADS:
                scores(h + 1)
            probs(h)
            if h >= 1:
                weighted(h - 1)
        weighted(ATT_HEADS - 1)

        yb = jnp.concatenate(
            [o_ref[:, 2 * p * V7X_LANES:(2 * p + 1) * V7X_LANES]
             + o_ref[:, (2 * p + 1) * V7X_LANES:(2 * p + 2) * V7X_LANES] for p in range(n_pairs)],
            axis=-1)
        ms = jnp.mean(yb * yb, axis=-1, keepdims=True)
        y_ref[qrows, GMLP_WIDTH:] = (yb * lax.rsqrt(ms + RMS_EPS) * gb_ref[...]).astype(BF16)

    band_tiles = BAND_BLOCK // KEY_TILE
    first_full = PAD // Q_BLOCK
    for b in range(first_full):
        attn_block(b * Q_BLOCK, tuple(range(band_tiles - 1 - b, band_tiles)))

    def full_block(b, carry):
        attn_block(pl.multiple_of(b * Q_BLOCK, Q_BLOCK), tuple(range(band_tiles)))
        return carry

    lax.fori_loop(first_full, seq // Q_BLOCK, full_block, 0)


def _mixer(u, vn, q, k, v, ws, bs, bias, ga, gb, seq):
    n = u.shape[0]
    row = pl.BlockSpec((seq, GMLP_WIDTH), lambda b: (b, 0))
    return pl.pallas_call(
        _mixer_kernel,
        grid=(n // seq,),
        in_specs=[row] * 5 + [_resident(ws.shape), _resident(bs.shape), _resident(bias.shape),
                              _resident(ga.shape), _resident(gb.shape)],
        out_specs=pl.BlockSpec((seq, GMLP_WIDTH + ATT_WIDTH), lambda b: (b, 0)),
        out_shape=jax.ShapeDtypeStruct((n, GMLP_WIDTH + ATT_WIDTH), BF16),
        scratch_shapes=[pltpu.VMEM(ws.shape, BF16),
                        pltpu.VMEM((seq, ATT_HEADS * V7X_LANES), BF16),
                        pltpu.VMEM((2, Q_BLOCK, BAND_BLOCK), F32),
                        pltpu.VMEM((2, Q_BLOCK, V7X_LANES), F32),
                        pltpu.VMEM((2, Q_BLOCK, BAND_BLOCK), BF16),
                        pltpu.VMEM((Q_BLOCK, ATT_HEADS * V7X_LANES), F32)],
        compiler_params=pltpu.CompilerParams(
            dimension_semantics=("parallel",), vmem_limit_bytes=V7X_VMEM_LIMIT_BYTES),
        name="mixer",
    )(u, vn, q, k, v, ws, bs, bias, ga, gb)


def _ffn_kernel(alpha, ff_chunk, y_ref, x_ref, wout_ref, w1_ref, w2_ref,
                g1_ref, b1_ref, g2_ref, b2_ref, o_ref):
    t = jnp.dot(y_ref[...], wout_ref[...], preferred_element_type=F32)
    h = _layer_norm_rows(alpha * x_ref[...] + t, g1_ref[...], b1_ref[...])
    hb = h.astype(BF16)
    acc = alpha * h
    for c in range(w1_ref.shape[1] // ff_chunk):
        cols = slice(c * ff_chunk, (c + 1) * ff_chunk)
        a = jnp.maximum(jnp.dot(hb, w1_ref[:, cols], preferred_element_type=F32), 0.0)
        acc = acc + jnp.dot((a * a).astype(BF16), w2_ref[cols, :], preferred_element_type=F32)
    o_ref[...] = _layer_norm_rows(acc, g2_ref[...], b2_ref[...])


def _ffn(y, x2, w_out, w1, w2, g1, b1, g2, b2, alpha, tm, ff_chunk):
    n, d = x2.shape
    tile = pl.BlockSpec((tm, d), lambda i: (i, 0))
    return pl.pallas_call(
        functools.partial(_ffn_kernel, alpha, ff_chunk),
        grid=(n // tm,),
        in_specs=[tile, tile, _resident(w_out.shape), _resident(w1.shape), _resident(w2.shape),
                  _resident(g1.shape), _resident(b1.shape), _resident(g2.shape),
                  _resident(b2.shape)],
        out_specs=tile,
        out_shape=jax.ShapeDtypeStruct((n, d), F32),
        compiler_params=pltpu.CompilerParams(
            dimension_semantics=("parallel",), vmem_limit_bytes=V7X_VMEM_LIMIT_BYTES),
        name="ffn",
    )(y, x2, w_out, w1, w2, g1, b1, g2, b2)


def _rel_bias_blocks(rel_table):
    heads = rel_table.shape[0]
    n_diag = Q_BLOCK + BAND_BLOCK - 1
    dist = PAD + (Q_BLOCK - 1) - np.arange(n_diag)
    diag = rel_table[:, np.clip(dist, -MAX_REL, MAX_REL) + MAX_REL].astype(F32) * LOG2E
    ext = jnp.pad(diag, ((0, 0), (0, 1)))
    skew = jnp.tile(ext, (1, Q_BLOCK))[:, :Q_BLOCK * n_diag].reshape(heads, Q_BLOCK, n_diag)
    toeplitz = skew[:, :, Q_BLOCK - 1:Q_BLOCK - 1 + BAND_BLOCK]
    q_chunk = np.arange(Q_BLOCK)[:, None] // CHUNK
    k_chunk = np.arange(BAND_BLOCK)[None, :] // CHUNK
    in_band = (k_chunk >= q_chunk) & (k_chunk <= q_chunk + N_PREV_CHUNKS)
    return jnp.where(in_band[None], toeplitz, -jnp.inf)


def kernel(x, w_in, gmlp_ln_g, gmlp_ln_b, w_spatial, b_spatial, rel_bias,
           gate_g_gmlp, gate_g_attn, w_out, ln1_g, ln1_b, w_ff1, w_ff2, ln2_g, ln2_b):
    batch, seq, d_model = x.shape
    depth = w_in.shape[0]
    alpha = (2.0 * depth) ** 0.25
    tm = 512

    avg_half = 2 * V7X_LANES
    group_of = np.arange(avg_half) // GMLP_GROUP_DIM
    avg = jnp.asarray((group_of[:, None] == group_of[None, :]) / GMLP_GROUP_DIM, BF16)

    h = x.reshape(batch * seq, d_model)
    for l in range(depth):
        u, vn, q, k, v = _in_proj(
            h, w_in[l].astype(BF16),
            gmlp_ln_g[l].reshape(1, GMLP_WIDTH), gmlp_ln_b[l].reshape(1, GMLP_WIDTH), avg, tm)
        ws = w_spatial[l].astype(BF16).reshape(GMLP_GROUPS // 2, 2 * GMLP_CHUNK, GMLP_CHUNK)
        bs = jnp.repeat(jnp.transpose(b_spatial[l]), GMLP_GROUP_DIM, axis=1)
        y = _mixer(u, vn, q, k, v, ws, bs, _rel_bias_blocks(rel_bias[l]),
                   gate_g_gmlp[l].reshape(1, GMLP_WIDTH), gate_g_attn[l].reshape(1, ATT_WIDTH), seq)
        h = _ffn(y, h, w_out[l].astype(BF16), w_ff1[l].astype(BF16), w_ff2[l].astype(BF16),
                 ln1_g[l].reshape(1, d_model), ln1_b[l].reshape(1, d_model),
                 ln2_g[l].reshape(1, d_model), ln2_b[l].reshape(1, d_model), alpha, tm, 1024)
    return h.reshape(batch, seq, d_model)
```

```python
import functools

import jax
import jax.numpy as jnp
import numpy as np
from jax import lax
from jax.experimental import pallas as pl
from jax.experimental.pallas import tpu as pltpu

CHUNK = 64
N_PREV_CHUNKS = 8
BAND = (N_PREV_CHUNKS + 1) * CHUNK
PAD = N_PREV_CHUNKS * CHUNK
ATT_HEADS = 8
HEAD_DIM = 64
ATT_WIDTH = ATT_HEADS * HEAD_DIM
MAX_REL = 256
GMLP_GROUPS = 8
GMLP_GROUP_DIM = 64
GMLP_WIDTH = GMLP_GROUPS * GMLP_GROUP_DIM
GMLP_CHUNK = 128
LN_EPS = 1e-5
RMS_EPS = 1e-6
LOG2E = 1.4426950408889634

Q_BLOCK = 256
KEY_TILE = 256
BAND_BLOCK = PAD + Q_BLOCK

V7X_LANES = 128
V7X_VMEM_LIMIT_BYTES = 56 * 1024 * 1024

BF16 = jnp.bfloat16
F32 = jnp.float32


def _layer_norm_rows(t, g, b):
    mu = jnp.mean(t, axis=-1, keepdims=True)
    d = t - mu
    var = jnp.mean(d * d, axis=-1, keepdims=True)
    return d * lax.rsqrt(var + LN_EPS) * g + b


def _resident(shape):
    return pl.BlockSpec(shape, lambda *_: (0,) * len(shape), pipeline_mode=pl.Buffered(1))


def _in_proj_kernel(x_ref, w_ref, lng_ref, lnb_ref, avg_ref, u_ref, vn_ref, q_ref, k_ref, v_ref):
    xb = x_ref[...].astype(BF16)

    def proj(col):
        return jnp.dot(xb, w_ref[:, col * GMLP_WIDTH:(col + 1) * GMLP_WIDTH],
                       preferred_element_type=F32)

    u_ref[...] = jax.nn.gelu(proj(0)).astype(BF16)

    gv = jax.nn.gelu(proj(1))
    avg = avg_ref[...]
    half = avg.shape[0]

    def group_mean(t):
        tb = t.astype(BF16)
        return jnp.concatenate(
            [jnp.dot(tb[:, i * half:(i + 1) * half], avg, preferred_element_type=F32)
             for i in range(GMLP_WIDTH // half)], axis=-1)

    d = gv - group_mean(gv)
    var = group_mean(d * d)
    vn_ref[...] = (d * lax.rsqrt(var + LN_EPS) * lng_ref[...] + lnb_ref[...]).astype(BF16)

    q_ref[...] = (proj(2) * (HEAD_DIM ** -0.5 * LOG2E)).astype(BF16)
    k_ref[...] = proj(3).astype(BF16)
    v_ref[...] = proj(4).astype(BF16)


def _in_proj(x2, w_in, ln_g, ln_b, avg, tm):
    n, d = x2.shape
    width = GMLP_WIDTH
    out = jax.ShapeDtypeStruct((n, width), BF16)
    tile = pl.BlockSpec((tm, width), lambda i: (i, 0))
    return pl.pallas_call(
        _in_proj_kernel,
        grid=(n // tm,),
        in_specs=[pl.BlockSpec((tm, d), lambda i: (i, 0)),
                  _resident(w_in.shape), _resident(ln_g.shape), _resident(ln_b.shape),
                  _resident(avg.shape)],
        out_specs=[tile] * 5,
        out_shape=[out] * 5,
        compiler_params=pltpu.CompilerParams(
            dimension_semantics=("parallel",), vmem_limit_bytes=V7X_VMEM_LIMIT_BYTES),
        name="in_proj",
    )(x2, w_in, ln_g, ln_b, avg)


def _mixer_kernel(u_ref, vn_ref, q_ref, k_ref, v_ref, ws_ref, bs_ref, bias_ref, ga_ref, gb_ref,
                  y_ref, wsm_ref, vx_ref, s_ref, m_ref, p_ref, o_ref):
    seq = u_ref.shape[0]
    n_pairs = ATT_WIDTH // V7X_LANES

    row = lax.broadcasted_iota(jnp.int32, (GMLP_CHUNK, GMLP_CHUNK), 0) // CHUNK
    col = lax.broadcasted_iota(jnp.int32, (GMLP_CHUNK, GMLP_CHUNK), 1) // CHUNK
    causal = jnp.concatenate([row >= col, row >= col], axis=0)
    first_group = lax.broadcasted_iota(jnp.int32, (GMLP_CHUNK, V7X_LANES), 1) < GMLP_GROUP_DIM
    for p in range(n_pairs):
        wsm_ref[p] = jnp.where(causal, ws_ref[p], jnp.zeros_like(ws_ref[p]))

    def gate_windows(row0):
        win = [pl.ds(row0 + i * GMLP_CHUNK, GMLP_CHUNK) for i in range(2)]
        parts = [[], []]
        for p in range(n_pairs):
            lanes = slice(p * V7X_LANES, (p + 1) * V7X_LANES)
            vals = jnp.concatenate([vn_ref[win[0], lanes], vn_ref[win[1], lanes]], axis=1)
            r = jnp.dot(wsm_ref[p], vals, preferred_element_type=F32)
            for i in range(2):
                cols = slice(i * V7X_LANES, (i + 1) * V7X_LANES)
                parts[i].append(jnp.where(first_group, r[:GMLP_CHUNK, cols], r[GMLP_CHUNK:, cols]))
        for i in range(2):
            vm = jnp.concatenate(parts[i], axis=-1) + bs_ref[...]
            ya = u_ref[win[i], :].astype(F32) * vm
            ms = jnp.mean(ya * ya, axis=-1, keepdims=True)
            y_ref[win[i], :GMLP_WIDTH] = (ya * lax.rsqrt(ms + RMS_EPS) * ga_ref[...]).astype(BF16)

    lane_full = lax.broadcasted_iota(jnp.int32, (seq, V7X_LANES), 1)
    for h in range(ATT_HEADS):
        pair = slice((h // 2) * V7X_LANES, (h // 2 + 1) * V7X_LANES)
        own = (lane_full < HEAD_DIM) if h % 2 == 0 else (lane_full >= HEAD_DIM)
        vx_ref[:, h * V7X_LANES:(h + 1) * V7X_LANES] = jnp.where(
            own, v_ref[:, pair], jnp.zeros((seq, V7X_LANES), BF16))

    lane = lax.broadcasted_iota(jnp.int32, (Q_BLOCK, V7X_LANES), 1)
    head_lanes = [lane < HEAD_DIM, lane >= HEAD_DIM]

    def attn_block(qrow0, tiles):
        nk = len(tiles) * KEY_TILE
        koff = tiles[0] * KEY_TILE
        qrows = pl.ds(qrow0, Q_BLOCK)
        krow0 = qrow0 - PAD + koff
        if not isinstance(krow0, int):
            krow0 = pl.multiple_of(krow0, KEY_TILE)
        krows = pl.ds(krow0, nk)

        def scores(h):
            pair = slice((h // 2) * V7X_LANES, (h // 2 + 1) * V7X_LANES)
            qp = q_ref[qrows, pair]
            qh = jnp.where(head_lanes[h % 2], qp, jnp.zeros_like(qp))
            s = lax.dot_general(qh, k_ref[krows, pair], (((1,), (1,)), ((), ())),
                                preferred_element_type=F32)
            s_ref[h % 2, :, :nk] = s + bias_ref[h, :, koff:koff + nk]

        def probs(h):
            m = jnp.max(s_ref[h % 2, :, :nk], axis=-1, keepdims=True)
            m_ref[h % 2] = jnp.broadcast_to(m, (Q_BLOCK, V7X_LANES))
            for t in range(nk // V7X_LANES):
                cols = slice(t * V7X_LANES, (t + 1) * V7X_LANES)
                p_ref[h % 2, :, cols] = jnp.exp2(s_ref[h % 2, :, cols] - m_ref[h % 2]).astype(BF16)

        def weighted(h):
            vals = jnp.concatenate([vx_ref[krows, h * V7X_LANES:(h + 1) * V7X_LANES],
                                    jnp.ones((nk, V7X_LANES), BF16)], axis=1)
            o = jnp.dot(p_ref[h % 2, :, :nk], vals, preferred_element_type=F32)
            o_ref[:, h * V7X_LANES:(h + 1) * V7X_LANES] = o[:, :V7X_LANES] / o[:, V7X_LANES:]

        gate_windows(qrow0)

        scores(0)
        for h in range(ATT_HEADS):
            if h + 1 < ATT_HEADS:
                scores(h + 1)
            probs(h)
            if h >= 1:
                weighted(h - 1)
        weighted(ATT_HEADS - 1)

        yb = jnp.concatenate(
            [o_ref[:, 2 * p * V7X_LANES:(2 * p + 1) * V7X_LANES]
             + o_ref[:, (2 * p + 1) * V7X_LANES:(2 * p + 2) * V7X_LANES] for p in range(n_pairs)],
            axis=-1)
        ms = jnp.mean(yb * yb, axis=-1, keepdims=True)
        y_ref[qrows, GMLP_WIDTH:] = (yb * lax.rsqrt(ms + RMS_EPS) * gb_ref[...]).astype(BF16)

    band_tiles = BAND_BLOCK // KEY_TILE
    first_full = PAD // Q_BLOCK
    for b in range(first_full):
        attn_block(b * Q_BLOCK, tuple(range(band_tiles - 1 - b, band_tiles)))

    def full_block(b, carry):
        attn_block(pl.multiple_of(b * Q_BLOCK, Q_BLOCK), tuple(range(band_tiles)))
        return carry

    lax.fori_loop(first_full, seq // Q_BLOCK, full_block, 0)


def _mixer(u, vn, q, k, v, ws, bs, bias, ga, gb, seq):
    n = u.shape[0]
    row = pl.BlockSpec((seq, GMLP_WIDTH), lambda b: (b, 0))
    return pl.pallas_call(
        _mixer_kernel,
        grid=(n // seq,),
        in_specs=[row] * 5 + [_resident(ws.shape), _resident(bs.shape), _resident(bias.shape),
                              _resident(ga.shape), _resident(gb.shape)],
        out_specs=pl.BlockSpec((seq, GMLP_WIDTH + ATT_WIDTH), lambda b: (b, 0)),
        out_shape=jax.ShapeDtypeStruct((n, GMLP_WIDTH + ATT_WIDTH), BF16),
        scratch_shapes=[pltpu.VMEM(ws.shape, BF16),
                        pltpu.VMEM((seq, ATT_HEADS * V7X_LANES), BF16),
                        pltpu.VMEM((2, Q_BLOCK, BAND_BLOCK), F32),
                        pltpu.VMEM((2, Q_BLOCK, V7X_LANES), F32),
                        pltpu.VMEM((2, Q_BLOCK, BAND_BLOCK), BF16),
                        pltpu.VMEM((Q_BLOCK, ATT_HEADS * V7X_LANES), F32)],
        compiler_params=pltpu.CompilerParams(
            dimension_semantics=("parallel",), vmem_limit_bytes=V7X_VMEM_LIMIT_BYTES),
        name="mixer",
    )(u, vn, q, k, v, ws, bs, bias, ga, gb)


FFN_PIPELINE_LAG = 2
FFN_ANCHORS_PER_CHUNK = 4


def _ffn_kernel(alpha, ff_chunk, y_ref, x_ref, wout_ref, w1_ref, w2_ref,
                g1_ref, b1_ref, g2_ref, b2_ref, o_ref, hb_ref, res_ref, t_ref, z0_ref, z1_ref):
    i = pl.program_id(0)

    @pl.when(i == 0)
    def _():
        hb_ref[1] = jnp.zeros(hb_ref.shape[1:], BF16)
        res_ref[1] = jnp.zeros(res_ref.shape[1:], F32)
        z0_ref[...] = jnp.zeros(z0_ref.shape, F32)
        z1_ref[...] = jnp.zeros(z1_ref.shape, F32)

    def step(par):
        z_in, z_out = (z0_ref, z1_ref) if par == 0 else (z1_ref, z0_ref)
        tm = o_ref.shape[0]
        n_chunks = w1_ref.shape[1] // ff_chunk
        rows_per = tm // (n_chunks * FFN_ANCHORS_PER_CHUNK)
        anchor_stride = ff_chunk // FFN_ANCHORS_PER_CHUNK
        t_ref[:tm, :] = jnp.dot(y_ref[...], wout_ref[...], preferred_element_type=F32)
        unknown_zero = jnp.minimum(i, 0)
        acc = res_ref[1 - par]
        for c in range(n_chunks):
            cols = slice(c * ff_chunk, (c + 1) * ff_chunk)
            a = jnp.dot(hb_ref[1 - par], w1_ref[:, cols], preferred_element_type=F32)
            for j in range(FFN_ANCHORS_PER_CHUNK):
                piece = a[:8, j * anchor_stride:j * anchor_stride + V7X_LANES]
                t_ref[tm:tm + 8, :V7X_LANES] = piece
                z_in[tm:tm + 8, :V7X_LANES] = piece
                first = (c * FFN_ANCHORS_PER_CHUNK + j) * rows_per
                rows = pl.ds(pl.multiple_of(first + unknown_zero, 8), rows_per)
                static_rows = slice(first, first + rows_per)
                h = _layer_norm_rows(alpha * x_ref[static_rows, :] + t_ref[rows, :],
                                     g1_ref[...], b1_ref[...])
                hb_ref[par, static_rows, :] = h.astype(BF16)
                res_ref[par, static_rows, :] = alpha * h
                o_ref[static_rows, :] = _layer_norm_rows(z_in[rows, :], g2_ref[...], b2_ref[...])
            a = jnp.maximum(a, 0.0)
            acc = acc + jnp.dot((a * a).astype(BF16), w2_ref[cols, :],
                                preferred_element_type=F32)
        z_out[:tm, :] = acc

    for par in range(2):
        pl.when(i % 2 == par)(functools.partial(step, par))


def _ffn(y, x2, w_out, w1, w2, g1, b1, g2, b2, alpha, tm, ff_chunk):
    n, d = x2.shape
    n_tiles = n // tm
    tile_in = pl.BlockSpec((tm, d), lambda i: (jnp.minimum(i, n_tiles - 1), 0))
    tile_out = pl.BlockSpec((tm, d), lambda i: (jnp.maximum(i - FFN_PIPELINE_LAG, 0), 0))
    return pl.pallas_call(
        functools.partial(_ffn_kernel, alpha, ff_chunk),
        grid=(n_tiles + FFN_PIPELINE_LAG,),
        in_specs=[tile_in, tile_in, _resident(w_out.shape), _resident(w1.shape),
                  _resident(w2.shape), _resident(g1.shape), _resident(b1.shape),
                  _resident(g2.shape), _resident(b2.shape)],
        out_specs=tile_out,
        out_shape=jax.ShapeDtypeStruct((n, d), F32),
        scratch_shapes=[pltpu.VMEM((2, tm, d), BF16),
                        pltpu.VMEM((2, tm, d), F32),
                        pltpu.VMEM((tm + 8, d), F32),
                        pltpu.VMEM((tm + 8, d), F32),
                        pltpu.VMEM((tm + 8, d), F32)],
        compiler_params=pltpu.CompilerParams(
            dimension_semantics=("arbitrary",), vmem_limit_bytes=V7X_VMEM_LIMIT_BYTES),
        name="ffn",
    )(y, x2, w_out, w1, w2, g1, b1, g2, b2)


def _rel_bias_blocks(rel_table):
    heads = rel_table.shape[0]
    n_diag = Q_BLOCK + BAND_BLOCK - 1
    dist = PAD + (Q_BLOCK - 1) - np.arange(n_diag)
    diag = rel_table[:, np.clip(dist, -MAX_REL, MAX_REL) + MAX_REL].astype(F32) * LOG2E
    ext = jnp.pad(diag, ((0, 0), (0, 1)))
    skew = jnp.tile(ext, (1, Q_BLOCK))[:, :Q_BLOCK * n_diag].reshape(heads, Q_BLOCK, n_diag)
    toeplitz = skew[:, :, Q_BLOCK - 1:Q_BLOCK - 1 + BAND_BLOCK]
    q_chunk = np.arange(Q_BLOCK)[:, None] // CHUNK
    k_chunk = np.arange(BAND_BLOCK)[None, :] // CHUNK
    in_band = (k_chunk >= q_chunk) & (k_chunk <= q_chunk + N_PREV_CHUNKS)
    return jnp.where(in_band[None], toeplitz, -jnp.inf)


def kernel(x, w_in, gmlp_ln_g, gmlp_ln_b, w_spatial, b_spatial, rel_bias,
           gate_g_gmlp, gate_g_attn, w_out, ln1_g, ln1_b, w_ff1, w_ff2, ln2_g, ln2_b):
    batch, seq, d_model = x.shape
    depth = w_in.shape[0]
    alpha = (2.0 * depth) ** 0.25
    tm = 512

    avg_half = 2 * V7X_LANES
    group_of = np.arange(avg_half) // GMLP_GROUP_DIM
    avg = jnp.asarray((group_of[:, None] == group_of[None, :]) / GMLP_GROUP_DIM, BF16)

    h = x.reshape(batch * seq, d_model)
    for l in range(depth):
        u, vn, q, k, v = _in_proj(
            h, w_in[l].astype(BF16),
            gmlp_ln_g[l].reshape(1, GMLP_WIDTH), gmlp_ln_b[l].reshape(1, GMLP_WIDTH), avg, tm)
        ws = w_spatial[l].astype(BF16).reshape(GMLP_GROUPS // 2, 2 * GMLP_CHUNK, GMLP_CHUNK)
        bs = jnp.repeat(jnp.transpose(b_spatial[l]), GMLP_GROUP_DIM, axis=1)
        y = _mixer(u, vn, q, k, v, ws, bs, _rel_bias_blocks(rel_bias[l]),
                   gate_g_gmlp[l].reshape(1, GMLP_WIDTH), gate_g_attn[l].reshape(1, ATT_WIDTH), seq)
        h = _ffn(y, h, w_out[l].astype(BF16), w_ff1[l].astype(BF16), w_ff2[l].astype(BF16),
                 ln1_g[l].reshape(1, d_model), ln1_b[l].reshape(1, d_model),
                 ln2_g[l].reshape(1, d_model), ln2_b[l].reshape(1, d_model), alpha, tm, 1024)
    return h.reshape(batch, seq, d_model)
```

```python
import functools

import jax
import jax.numpy as jnp
import numpy as np
from jax import lax
from jax.experimental import pallas as pl
from jax.experimental.pallas import tpu as pltpu

CHUNK = 64
N_PREV_CHUNKS = 8
BAND = (N_PREV_CHUNKS + 1) * CHUNK
PAD = N_PREV_CHUNKS * CHUNK
ATT_HEADS = 8
HEAD_DIM = 64
ATT_WIDTH = ATT_HEADS * HEAD_DIM
MAX_REL = 256
GMLP_GROUPS = 8
GMLP_GROUP_DIM = 64
GMLP_WIDTH = GMLP_GROUPS * GMLP_GROUP_DIM
GMLP_CHUNK = 128
LN_EPS = 1e-5
RMS_EPS = 1e-6
LOG2E = 1.4426950408889634

Q_BLOCK = 256
KEY_TILE = 256
BAND_BLOCK = PAD + Q_BLOCK
BLOCKS_PER_TRIP = 3

V7X_LANES = 128
V7X_VMEM_LIMIT_BYTES = 56 * 1024 * 1024

BF16 = jnp.bfloat16
F32 = jnp.float32


def _layer_norm_rows(t, g, b):
    mu = jnp.mean(t, axis=-1, keepdims=True)
    d = t - mu
    var = jnp.mean(d * d, axis=-1, keepdims=True)
    return d * lax.rsqrt(var + LN_EPS) * g + b


def _resident(shape):
    return pl.BlockSpec(shape, lambda *_: (0,) * len(shape), pipeline_mode=pl.Buffered(1))


def _in_proj_kernel(x_ref, w_ref, lng_ref, lnb_ref, avg_ref, u_ref, vn_ref, q_ref, k_ref, v_ref):
    xb = x_ref[...].astype(BF16)

    def proj(col):
        return jnp.dot(xb, w_ref[:, col * GMLP_WIDTH:(col + 1) * GMLP_WIDTH],
                       preferred_element_type=F32)

    u_ref[...] = jax.nn.gelu(proj(0)).astype(BF16)

    gv = jax.nn.gelu(proj(1))
    avg = avg_ref[...]
    half = avg.shape[0]

    def group_mean(t):
        tb = t.astype(BF16)
        return jnp.concatenate(
            [jnp.dot(tb[:, i * half:(i + 1) * half], avg, preferred_element_type=F32)
             for i in range(GMLP_WIDTH // half)], axis=-1)

    d = gv - group_mean(gv)
    var = group_mean(d * d)
    vn_ref[...] = (d * lax.rsqrt(var + LN_EPS) * lng_ref[...] + lnb_ref[...]).astype(BF16)

    q_ref[...] = (proj(2) * (HEAD_DIM ** -0.5 * LOG2E)).astype(BF16)
    k_ref[...] = proj(3).astype(BF16)
    v_ref[...] = proj(4).astype(BF16)


def _in_proj(x2, w_in, ln_g, ln_b, avg, tm):
    n, d = x2.shape
    width = GMLP_WIDTH
    out = jax.ShapeDtypeStruct((n, width), BF16)
    tile = pl.BlockSpec((tm, width), lambda i: (i, 0))
    return pl.pallas_call(
        _in_proj_kernel,
        grid=(n // tm,),
        in_specs=[pl.BlockSpec((tm, d), lambda i: (i, 0)),
                  _resident(w_in.shape), _resident(ln_g.shape), _resident(ln_b.shape),
                  _resident(avg.shape)],
        out_specs=[tile] * 5,
        out_shape=[out] * 5,
        compiler_params=pltpu.CompilerParams(
            dimension_semantics=("parallel",), vmem_limit_bytes=V7X_VMEM_LIMIT_BYTES),
        name="in_proj",
    )(x2, w_in, ln_g, ln_b, avg)


def _mixer_kernel(u_ref, vn_ref, q_ref, k_ref, v_ref, ws_ref, bs_ref, bias_ref, ga_ref, gb_ref,
                  y_ref, wsm_ref, vx_ref, s_ref, m_ref, p_ref, o_ref):
    seq = u_ref.shape[0]
    n_pairs = ATT_WIDTH // V7X_LANES

    row = lax.broadcasted_iota(jnp.int32, (GMLP_CHUNK, GMLP_CHUNK), 0) // CHUNK
    col = lax.broadcasted_iota(jnp.int32, (GMLP_CHUNK, GMLP_CHUNK), 1) // CHUNK
    causal = jnp.concatenate([row >= col, row >= col], axis=0)
    first_group = lax.broadcasted_iota(jnp.int32, (GMLP_CHUNK, V7X_LANES), 1) < GMLP_GROUP_DIM
    for p in range(n_pairs):
        wsm_ref[p] = jnp.where(causal, ws_ref[p], jnp.zeros_like(ws_ref[p]))

    def gate_windows(row0):
        win = [pl.ds(row0 + i * GMLP_CHUNK, GMLP_CHUNK) for i in range(2)]
        parts = [[], []]
        for p in range(n_pairs):
            lanes = slice(p * V7X_LANES, (p + 1) * V7X_LANES)
            vals = jnp.concatenate([vn_ref[win[0], lanes], vn_ref[win[1], lanes]], axis=1)
            r = jnp.dot(wsm_ref[p], vals, preferred_element_type=F32)
            for i in range(2):
                cols = slice(i * V7X_LANES, (i + 1) * V7X_LANES)
                parts[i].append(jnp.where(first_group, r[:GMLP_CHUNK, cols], r[GMLP_CHUNK:, cols]))
        for i in range(2):
            vm = jnp.concatenate(parts[i], axis=-1) + bs_ref[...]
            ya = u_ref[win[i], :].astype(F32) * vm
            ms = jnp.mean(ya * ya, axis=-1, keepdims=True)
            y_ref[win[i], :GMLP_WIDTH] = (ya * lax.rsqrt(ms + RMS_EPS) * ga_ref[...]).astype(BF16)

    lane_full = lax.broadcasted_iota(jnp.int32, (seq, V7X_LANES), 1)
    for h in range(ATT_HEADS):
        pair = slice((h // 2) * V7X_LANES, (h // 2 + 1) * V7X_LANES)
        own = (lane_full < HEAD_DIM) if h % 2 == 0 else (lane_full >= HEAD_DIM)
        vx_ref[:, h * V7X_LANES:(h + 1) * V7X_LANES] = jnp.where(
            own, v_ref[:, pair], jnp.zeros((seq, V7X_LANES), BF16))

    lane = lax.broadcasted_iota(jnp.int32, (Q_BLOCK, V7X_LANES), 1)
    head_lanes = [lane < HEAD_DIM, lane >= HEAD_DIM]

    def attn_block(qrow0, tiles):
        nk = len(tiles) * KEY_TILE
        koff = tiles[0] * KEY_TILE
        qrows = pl.ds(qrow0, Q_BLOCK)
        krow0 = qrow0 - PAD + koff
        if not isinstance(krow0, int):
            krow0 = pl.multiple_of(krow0, KEY_TILE)
        krows = pl.ds(krow0, nk)

        def scores(h):
            pair = slice((h // 2) * V7X_LANES, (h // 2 + 1) * V7X_LANES)
            qp = q_ref[qrows, pair]
            qh = jnp.where(head_lanes[h % 2], qp, jnp.zeros_like(qp))
            s = lax.dot_general(qh, k_ref[krows, pair], (((1,), (1,)), ((), ())),
                                preferred_element_type=F32)
            s_ref[h % 2, :, :nk] = s + bias_ref[h, :, koff:koff + nk]

        def probs(h):
            m = jnp.max(s_ref[h % 2, :, :nk], axis=-1, keepdims=True)
            m_ref[h % 2] = jnp.broadcast_to(m, (Q_BLOCK, V7X_LANES))
            for t in range(nk // V7X_LANES):
                cols = slice(t * V7X_LANES, (t + 1) * V7X_LANES)
                p_ref[h % 2, :, cols] = jnp.exp2(s_ref[h % 2, :, cols] - m_ref[h % 2]).astype(BF16)

        def weighted(h):
            vals = jnp.concatenate([vx_ref[krows, h * V7X_LANES:(h + 1) * V7X_LANES],
                                    jnp.ones((nk, V7X_LANES), BF16)], axis=1)
            o = jnp.dot(p_ref[h % 2, :, :nk], vals, preferred_element_type=F32)
            o_ref[:, h * V7X_LANES:(h + 1) * V7X_LANES] = o[:, :V7X_LANES] / o[:, V7X_LANES:]

        gate_windows(qrow0)

        scores(0)
        for h in range(ATT_HEADS):
            if h + 1 < ATT_HEADS:
                scores(h + 1)
            probs(h)
            if h >= 1:
                weighted(h - 1)
        weighted(ATT_HEADS - 1)

        yb = jnp.concatenate(
            [o_ref[:, 2 * p * V7X_LANES:(2 * p + 1) * V7X_LANES]
             + o_ref[:, (2 * p + 1) * V7X_LANES:(2 * p + 2) * V7X_LANES] for p in range(n_pairs)],
            axis=-1)
        ms = jnp.mean(yb * yb, axis=-1, keepdims=True)
        y_ref[qrows, GMLP_WIDTH:] = (yb * lax.rsqrt(ms + RMS_EPS) * gb_ref[...]).astype(BF16)

    band_tiles = BAND_BLOCK // KEY_TILE
    first_full = PAD // Q_BLOCK
    for b in range(first_full):
        attn_block(b * Q_BLOCK, tuple(range(band_tiles - 1 - b, band_tiles)))

    def full_block(b, carry):
        attn_block(pl.multiple_of(b * Q_BLOCK, Q_BLOCK), tuple(range(band_tiles)))
        return carry

    lax.fori_loop(first_full, seq // Q_BLOCK, full_block, 0, unroll=BLOCKS_PER_TRIP)


def _mixer(u, vn, q, k, v, ws, bs, bias, ga, gb, seq):
    n = u.shape[0]
    row = pl.BlockSpec((seq, GMLP_WIDTH), lambda b: (b, 0))
    return pl.pallas_call(
        _mixer_kernel,
        grid=(n // seq,),
        in_specs=[row] * 5 + [_resident(ws.shape), _resident(bs.shape), _resident(bias.shape),
                              _resident(ga.shape), _resident(gb.shape)],
        out_specs=pl.BlockSpec((seq, GMLP_WIDTH + ATT_WIDTH), lambda b: (b, 0)),
        out_shape=jax.ShapeDtypeStruct((n, GMLP_WIDTH + ATT_WIDTH), BF16),
        scratch_shapes=[pltpu.VMEM(ws.shape, BF16),
                        pltpu.VMEM((seq, ATT_HEADS * V7X_LANES), BF16),
                        pltpu.VMEM((2, Q_BLOCK, BAND_BLOCK), F32),
                        pltpu.VMEM((2, Q_BLOCK, V7X_LANES), F32),
                        pltpu.VMEM((2, Q_BLOCK, BAND_BLOCK), BF16),
                        pltpu.VMEM((Q_BLOCK, ATT_HEADS * V7X_LANES), F32)],
        compiler_params=pltpu.CompilerParams(
            dimension_semantics=("parallel",), vmem_limit_bytes=V7X_VMEM_LIMIT_BYTES),
        name="mixer",
    )(u, vn, q, k, v, ws, bs, bias, ga, gb)


FFN_PIPELINE_LAG = 2
FFN_ANCHORS_PER_CHUNK = 4


def _ffn_kernel(alpha, ff_chunk, y_ref, x_ref, wout_ref, w1_ref, w2_ref,
                g1_ref, b1_ref, g2_ref, b2_ref, o_ref, hb_ref, res_ref, t_ref, z0_ref, z1_ref):
    i = pl.program_id(0)

    @pl.when(i == 0)
    def _():
        hb_ref[1] = jnp.zeros(hb_ref.shape[1:], BF16)
        res_ref[1] = jnp.zeros(res_ref.shape[1:], F32)
        z0_ref[...] = jnp.zeros(z0_ref.shape, F32)
        z1_ref[...] = jnp.zeros(z1_ref.shape, F32)

    def step(par):
        z_in, z_out = (z0_ref, z1_ref) if par == 0 else (z1_ref, z0_ref)
        tm = o_ref.shape[0]
        n_chunks = w1_ref.shape[1] // ff_chunk
        rows_per = tm // (n_chunks * FFN_ANCHORS_PER_CHUNK)
        anchor_stride = ff_chunk // FFN_ANCHORS_PER_CHUNK
        t_ref[:tm, :] = jnp.dot(y_ref[...], wout_ref[...], preferred_element_type=F32)
        unknown_zero = jnp.minimum(i, 0)
        acc = res_ref[1 - par]
        for c in range(n_chunks):
            cols = slice(c * ff_chunk, (c + 1) * ff_chunk)
            a = jnp.dot(hb_ref[1 - par], w1_ref[:, cols], preferred_element_type=F32)
            for j in range(FFN_ANCHORS_PER_CHUNK):
                piece = a[:8, j * anchor_stride:j * anchor_stride + V7X_LANES]
                t_ref[tm:tm + 8, :V7X_LANES] = piece
                z_in[tm:tm + 8, :V7X_LANES] = piece
                first = (c * FFN_ANCHORS_PER_CHUNK + j) * rows_per
                rows = pl.ds(pl.multiple_of(first + unknown_zero, 8), rows_per)
                static_rows = slice(first, first + rows_per)
                h = _layer_norm_rows(alpha * x_ref[static_rows, :] + t_ref[rows, :],
                                     g1_ref[...], b1_ref[...])
                hb_ref[par, static_rows, :] = h.astype(BF16)
                res_ref[par, static_rows, :] = alpha * h
                o_ref[static_rows, :] = _layer_norm_rows(z_in[rows, :], g2_ref[...], b2_ref[...])
            a = jnp.maximum(a, 0.0)
            acc = acc + jnp.dot((a * a).astype(BF16), w2_ref[cols, :],
                                preferred_element_type=F32)
        z_out[:tm, :] = acc

    for par in range(2):
        pl.when(i % 2 == par)(functools.partial(step, par))


def _ffn(y, x2, w_out, w1, w2, g1, b1, g2, b2, alpha, tm, ff_chunk):
    n, d = x2.shape
    n_tiles = n // tm
    tile_in = pl.BlockSpec((tm, d), lambda i: (jnp.minimum(i, n_tiles - 1), 0))
    tile_out = pl.BlockSpec((tm, d), lambda i: (jnp.maximum(i - FFN_PIPELINE_LAG, 0), 0))
    return pl.pallas_call(
        functools.partial(_ffn_kernel, alpha, ff_chunk),
        grid=(n_tiles + FFN_PIPELINE_LAG,),
        in_specs=[tile_in, tile_in, _resident(w_out.shape), _resident(w1.shape),
                  _resident(w2.shape), _resident(g1.shape), _resident(b1.shape),
                  _resident(g2.shape), _resident(b2.shape)],
        out_specs=tile_out,
        out_shape=jax.ShapeDtypeStruct((n, d), F32),
        scratch_shapes=[pltpu.VMEM((2, tm, d), BF16),
                        pltpu.VMEM((2, tm, d), F32),
                        pltpu.VMEM((tm + 8, d), F32),
                        pltpu.VMEM((tm + 8, d), F32),
                        pltpu.VMEM((tm + 8, d), F32)],
        compiler_params=pltpu.CompilerParams(
            dimension_semantics=("arbitrary",), vmem_limit_bytes=V7X_VMEM_LIMIT_BYTES),
        name="ffn",
    )(y, x2, w_out, w1, w2, g1, b1, g2, b2)


def _rel_bias_blocks(rel_table):
    heads = rel_table.shape[0]
    n_diag = Q_BLOCK + BAND_BLOCK - 1
    dist = PAD + (Q_BLOCK - 1) - np.arange(n_diag)
    diag = rel_table[:, np.clip(dist, -MAX_REL, MAX_REL) + MAX_REL].astype(F32) * LOG2E
    ext = jnp.pad(diag, ((0, 0), (0, 1)))
    skew = jnp.tile(ext, (1, Q_BLOCK))[:, :Q_BLOCK * n_diag].reshape(heads, Q_BLOCK, n_diag)
    toeplitz = skew[:, :, Q_BLOCK - 1:Q_BLOCK - 1 + BAND_BLOCK]
    q_chunk = np.arange(Q_BLOCK)[:, None] // CHUNK
    k_chunk = np.arange(BAND_BLOCK)[None, :] // CHUNK
    in_band = (k_chunk >= q_chunk) & (k_chunk <= q_chunk + N_PREV_CHUNKS)
    return jnp.where(in_band[None], toeplitz, -jnp.inf)


def kernel(x, w_in, gmlp_ln_g, gmlp_ln_b, w_spatial, b_spatial, rel_bias,
           gate_g_gmlp, gate_g_attn, w_out, ln1_g, ln1_b, w_ff1, w_ff2, ln2_g, ln2_b):
    batch, seq, d_model = x.shape
    depth = w_in.shape[0]
    alpha = (2.0 * depth) ** 0.25
    tm = 512

    avg_half = 2 * V7X_LANES
    group_of = np.arange(avg_half) // GMLP_GROUP_DIM
    avg = jnp.asarray((group_of[:, None] == group_of[None, :]) / GMLP_GROUP_DIM, BF16)

    h = x.reshape(batch * seq, d_model)
    for l in range(depth):
        u, vn, q, k, v = _in_proj(
            h, w_in[l].astype(BF16),
            gmlp_ln_g[l].reshape(1, GMLP_WIDTH), gmlp_ln_b[l].reshape(1, GMLP_WIDTH), avg, tm)
        ws = w_spatial[l].astype(BF16).reshape(GMLP_GROUPS // 2, 2 * GMLP_CHUNK, GMLP_CHUNK)
        bs = jnp.repeat(jnp.transpose(b_spatial[l]), GMLP_GROUP_DIM, axis=1)
        y = _mixer(u, vn, q, k, v, ws, bs, _rel_bias_blocks(rel_bias[l]),
                   gate_g_gmlp[l].reshape(1, GMLP_WIDTH), gate_g_attn[l].reshape(1, ATT_WIDTH), seq)
        h = _ffn(y, h, w_out[l].astype(BF16), w_ff1[l].astype(BF16), w_ff2[l].astype(BF16),
                 ln1_g[l].reshape(1, d_model), ln1_b[l].reshape(1, d_model),
                 ln2_g[l].reshape(1, d_model), ln2_b[l].reshape(1, d_model), alpha, tm, 1024)
    return h.reshape(batch, seq, d_model)
```

```python
import functools

import jax
import jax.numpy as jnp
import numpy as np
from jax import lax
from jax.experimental import pallas as pl
from jax.experimental.pallas import tpu as pltpu

CHUNK = 64
N_PREV_CHUNKS = 8
BAND = (N_PREV_CHUNKS + 1) * CHUNK
PAD = N_PREV_CHUNKS * CHUNK
ATT_HEADS = 8
HEAD_DIM = 64
ATT_WIDTH = ATT_HEADS * HEAD_DIM
MAX_REL = 256
GMLP_GROUPS = 8
GMLP_GROUP_DIM = 64
GMLP_WIDTH = GMLP_GROUPS * GMLP_GROUP_DIM
GMLP_CHUNK = 128
LN_EPS = 1e-5
RMS_EPS = 1e-6
LOG2E = 1.4426950408889634

Q_BLOCK = 256
KEY_TILE = 256
BAND_BLOCK = PAD + Q_BLOCK
BLOCKS_PER_TRIP = 3

V7X_LANES = 128
V7X_VMEM_LIMIT_BYTES = 56 * 1024 * 1024

BF16 = jnp.bfloat16
F32 = jnp.float32


def _layer_norm_rows(t, g, b):
    mu = jnp.mean(t, axis=-1, keepdims=True)
    d = t - mu
    var = jnp.mean(d * d, axis=-1, keepdims=True)
    return d * lax.rsqrt(var + LN_EPS) * g + b


def _resident(shape):
    return pl.BlockSpec(shape, lambda *_: (0,) * len(shape), pipeline_mode=pl.Buffered(1))


def _in_proj_kernel(x_ref, w_ref, lng_ref, lnb_ref, avg_ref, u_ref, vn_ref, q_ref, k_ref, v_ref):
    xb = x_ref[...].astype(BF16)

    def proj(col):
        return jnp.dot(xb, w_ref[:, col * GMLP_WIDTH:(col + 1) * GMLP_WIDTH],
                       preferred_element_type=F32)

    u_ref[...] = jax.nn.gelu(proj(0)).astype(BF16)

    gv = jax.nn.gelu(proj(1))
    avg = avg_ref[...]
    half = avg.shape[0]

    def group_mean(t):
        tb = t.astype(BF16)
        return jnp.concatenate(
            [jnp.dot(tb[:, i * half:(i + 1) * half], avg, preferred_element_type=F32)
             for i in range(GMLP_WIDTH // half)], axis=-1)

    d = gv - group_mean(gv)
    var = group_mean(d * d)
    vn_ref[...] = (d * lax.rsqrt(var + LN_EPS) * lng_ref[...] + lnb_ref[...]).astype(BF16)

    q_ref[...] = (proj(2) * (HEAD_DIM ** -0.5 * LOG2E)).astype(BF16)
    k_ref[...] = proj(3).astype(BF16)
    v_ref[...] = proj(4).astype(BF16)


def _in_proj(x2, w_in, ln_g, ln_b, avg, tm):
    n, d = x2.shape
    width = GMLP_WIDTH
    out = jax.ShapeDtypeStruct((n, width), BF16)
    tile = pl.BlockSpec((tm, width), lambda i: (i, 0))
    return pl.pallas_call(
        _in_proj_kernel,
        grid=(n // tm,),
        in_specs=[pl.BlockSpec((tm, d), lambda i: (i, 0)),
                  _resident(w_in.shape), _resident(ln_g.shape), _resident(ln_b.shape),
                  _resident(avg.shape)],
        out_specs=[tile] * 5,
        out_shape=[out] * 5,
        compiler_params=pltpu.CompilerParams(
            dimension_semantics=("parallel",), vmem_limit_bytes=V7X_VMEM_LIMIT_BYTES),
        name="in_proj",
    )(x2, w_in, ln_g, ln_b, avg)


def _mixer_kernel(u_ref, vn_ref, q_ref, k_ref, v_ref, ws_ref, bs_ref, bias_ref, ga_ref, gb_ref,
                  y_ref, wsm_ref, vx_ref, s_ref, m_ref, p_ref, o_ref):
    seq = u_ref.shape[0]
    n_pairs = ATT_WIDTH // V7X_LANES

    row = lax.broadcasted_iota(jnp.int32, (GMLP_CHUNK, GMLP_CHUNK), 0) // CHUNK
    col = lax.broadcasted_iota(jnp.int32, (GMLP_CHUNK, GMLP_CHUNK), 1) // CHUNK
    causal = jnp.concatenate([row >= col, row >= col], axis=0)
    first_group = lax.broadcasted_iota(jnp.int32, (GMLP_CHUNK, V7X_LANES), 1) < GMLP_GROUP_DIM
    for p in range(n_pairs):
        wsm_ref[p] = jnp.where(causal, ws_ref[p], jnp.zeros_like(ws_ref[p]))

    def gate_windows(row0):
        win = [pl.ds(row0 + i * GMLP_CHUNK, GMLP_CHUNK) for i in range(2)]
        parts = [[], []]
        for p in range(n_pairs):
            lanes = slice(p * V7X_LANES, (p + 1) * V7X_LANES)
            vals = jnp.concatenate([vn_ref[win[0], lanes], vn_ref[win[1], lanes]], axis=1)
            r = jnp.dot(wsm_ref[p], vals, preferred_element_type=F32)
            for i in range(2):
                cols = slice(i * V7X_LANES, (i + 1) * V7X_LANES)
                parts[i].append(jnp.where(first_group, r[:GMLP_CHUNK, cols], r[GMLP_CHUNK:, cols]))
        for i in range(2):
            vm = jnp.concatenate(parts[i], axis=-1) + bs_ref[...]
            ya = u_ref[win[i], :].astype(F32) * vm
            ms = jnp.mean(ya * ya, axis=-1, keepdims=True)
            y_ref[win[i], :GMLP_WIDTH] = (ya * lax.rsqrt(ms + RMS_EPS) * ga_ref[...]).astype(BF16)

    lane = lax.broadcasted_iota(jnp.int32, (Q_BLOCK, V7X_LANES), 1)
    head_lanes = [lane < HEAD_DIM, lane >= HEAD_DIM]

    def expand_values(rows):
        for h in range(ATT_HEADS):
            pair = slice((h // 2) * V7X_LANES, (h // 2 + 1) * V7X_LANES)
            vp = v_ref[rows, pair]
            vx_ref[rows, h * V7X_LANES:(h + 1) * V7X_LANES] = jnp.where(
                head_lanes[h % 2], vp, jnp.zeros_like(vp))

    def attn_block(qrow0, tiles):
        nk = len(tiles) * KEY_TILE
        koff = tiles[0] * KEY_TILE
        qrows = pl.ds(qrow0, Q_BLOCK)
        krow0 = qrow0 - PAD + koff
        if not isinstance(krow0, int):
            krow0 = pl.multiple_of(krow0, KEY_TILE)
        krows = pl.ds(krow0, nk)

        def scores(h):
            pair = slice((h // 2) * V7X_LANES, (h // 2 + 1) * V7X_LANES)
            qp = q_ref[qrows, pair]
            qh = jnp.where(head_lanes[h % 2], qp, jnp.zeros_like(qp))
            s = lax.dot_general(qh, k_ref[krows, pair], (((1,), (1,)), ((), ())),
                                preferred_element_type=F32)
            s_ref[h % 2, :, :nk] = s + bias_ref[h, :, koff:koff + nk]

        def probs(h):
            m = jnp.max(s_ref[h % 2, :, :nk], axis=-1, keepdims=True)
            m_ref[h % 2] = jnp.broadcast_to(m, (Q_BLOCK, V7X_LANES))
            for t in range(nk // V7X_LANES):
                cols = slice(t * V7X_LANES, (t + 1) * V7X_LANES)
                p_ref[h % 2, :, cols] = jnp.exp2(s_ref[h % 2, :, cols] - m_ref[h % 2]).astype(BF16)

        def weighted(h):
            vals = jnp.concatenate([vx_ref[krows, h * V7X_LANES:(h + 1) * V7X_LANES],
                                    jnp.ones((nk, V7X_LANES), BF16)], axis=1)
            o = jnp.dot(p_ref[h % 2, :, :nk], vals, preferred_element_type=F32)
            o_ref[:, h * V7X_LANES:(h + 1) * V7X_LANES] = o[:, :V7X_LANES] / o[:, V7X_LANES:]

        expand_values(qrows)
        gate_windows(qrow0)

        scores(0)
        for h in range(ATT_HEADS):
            if h + 1 < ATT_HEADS:
                scores(h + 1)
            probs(h)
            if h >= 1:
                weighted(h - 1)
        weighted(ATT_HEADS - 1)

        yb = jnp.concatenate(
            [o_ref[:, 2 * p * V7X_LANES:(2 * p + 1) * V7X_LANES]
             + o_ref[:, (2 * p + 1) * V7X_LANES:(2 * p + 2) * V7X_LANES] for p in range(n_pairs)],
            axis=-1)
        ms = jnp.mean(yb * yb, axis=-1, keepdims=True)
        y_ref[qrows, GMLP_WIDTH:] = (yb * lax.rsqrt(ms + RMS_EPS) * gb_ref[...]).astype(BF16)

    band_tiles = BAND_BLOCK // KEY_TILE
    first_full = PAD // Q_BLOCK
    for b in range(first_full):
        attn_block(b * Q_BLOCK, tuple(range(band_tiles - 1 - b, band_tiles)))

    def full_block(b, carry):
        attn_block(pl.multiple_of(b * Q_BLOCK, Q_BLOCK), tuple(range(band_tiles)))
        return carry

    lax.fori_loop(first_full, seq // Q_BLOCK, full_block, 0, unroll=BLOCKS_PER_TRIP)


def _mixer(u, vn, q, k, v, ws, bs, bias, ga, gb, seq):
    n = u.shape[0]
    row = pl.BlockSpec((seq, GMLP_WIDTH), lambda b: (b, 0))
    return pl.pallas_call(
        _mixer_kernel,
        grid=(n // seq,),
        in_specs=[row] * 5 + [_resident(ws.shape), _resident(bs.shape), _resident(bias.shape),
                              _resident(ga.shape), _resident(gb.shape)],
        out_specs=pl.BlockSpec((seq, GMLP_WIDTH + ATT_WIDTH), lambda b: (b, 0)),
        out_shape=jax.ShapeDtypeStruct((n, GMLP_WIDTH + ATT_WIDTH), BF16),
        scratch_shapes=[pltpu.VMEM(ws.shape, BF16),
                        pltpu.VMEM((seq, ATT_HEADS * V7X_LANES), BF16),
                        pltpu.VMEM((2, Q_BLOCK, BAND_BLOCK), F32),
                        pltpu.VMEM((2, Q_BLOCK, V7X_LANES), F32),
                        pltpu.VMEM((2, Q_BLOCK, BAND_BLOCK), BF16),
                        pltpu.VMEM((Q_BLOCK, ATT_HEADS * V7X_LANES), F32)],
        compiler_params=pltpu.CompilerParams(
            dimension_semantics=("parallel",), vmem_limit_bytes=V7X_VMEM_LIMIT_BYTES),
        name="mixer",
    )(u, vn, q, k, v, ws, bs, bias, ga, gb)


FFN_PIPELINE_LAG = 2
FFN_ANCHORS_PER_CHUNK = 4
FFN_NORM_PIECE_ROWS = 16


def _ffn_kernel(alpha, ff_chunk, y_ref, x_ref, wout_ref, w1_ref, w2_ref,
                g1_ref, b1_ref, g2_ref, b2_ref, o_ref, hb_ref, res_ref, t_ref, z0_ref, z1_ref):
    i = pl.program_id(0)

    @pl.when(i == 0)
    def _():
        hb_ref[1] = jnp.zeros(hb_ref.shape[1:], BF16)
        res_ref[1] = jnp.zeros(res_ref.shape[1:], F32)
        z0_ref[...] = jnp.zeros(z0_ref.shape, F32)
        z1_ref[...] = jnp.zeros(z1_ref.shape, F32)

    def step(par):
        z_in, z_out = (z0_ref, z1_ref) if par == 0 else (z1_ref, z0_ref)
        tm = o_ref.shape[0]
        n_chunks = w1_ref.shape[1] // ff_chunk
        piece_index = 0
        t_ref[:tm, :] = jnp.dot(y_ref[...], wout_ref[...], preferred_element_type=F32)
        unknown_zero = jnp.minimum(i, 0)
        acc = res_ref[1 - par]
        for c in range(n_chunks):
            cols = slice(c * ff_chunk, (c + 1) * ff_chunk)
            a = jnp.dot(hb_ref[1 - par], w1_ref[:, cols], preferred_element_type=F32)
            def norm_pieces(fresh, piece_index, count):
                stride = fresh.shape[1] // FFN_ANCHORS_PER_CHUNK
                for j in range(FFN_ANCHORS_PER_CHUNK):
                    piece = fresh[:8, j * stride:j * stride + V7X_LANES]
                    t_ref[tm:tm + 8, :V7X_LANES] = piece
                    z_in[tm:tm + 8, :V7X_LANES] = piece
                    here = count // FFN_ANCHORS_PER_CHUNK + (j < count % FFN_ANCHORS_PER_CHUNK)
                    for _ in range(here):
                        first = piece_index * FFN_NORM_PIECE_ROWS
                        piece_index += 1
                        rows = pl.ds(pl.multiple_of(first + unknown_zero, 8),
                                     FFN_NORM_PIECE_ROWS)
                        static_rows = slice(first, first + FFN_NORM_PIECE_ROWS)
                        h = _layer_norm_rows(alpha * x_ref[static_rows, :] + t_ref[rows, :],
                                             g1_ref[...], b1_ref[...])
                        hb_ref[par, static_rows, :] = h.astype(BF16)
                        res_ref[par, static_rows, :] = alpha * h
                        o_ref[static_rows, :] = _layer_norm_rows(
                            z_in[rows, :], g2_ref[...], b2_ref[...])
                return piece_index

            groups_left = 2 * (n_chunks - c) - 1
            pieces_left = tm // FFN_NORM_PIECE_ROWS - piece_index
            piece_index = norm_pieces(a, piece_index, -(-pieces_left // groups_left))
            a = jnp.maximum(a, 0.0)
            acc = acc + jnp.dot((a * a).astype(BF16), w2_ref[cols, :],
                                preferred_element_type=F32)
            if groups_left > 1:
                pieces_left = tm // FFN_NORM_PIECE_ROWS - piece_index
                piece_index = norm_pieces(acc, piece_index, -(-pieces_left // (groups_left - 1)))
        z_out[:tm, :] = acc

    for par in range(2):
        pl.when(i % 2 == par)(functools.partial(step, par))


def _ffn(y, x2, w_out, w1, w2, g1, b1, g2, b2, alpha, tm, ff_chunk):
    n, d = x2.shape
    n_tiles = n // tm
    tile_in = pl.BlockSpec((tm, d), lambda i: (jnp.minimum(i, n_tiles - 1), 0))
    tile_out = pl.BlockSpec((tm, d), lambda i: (jnp.maximum(i - FFN_PIPELINE_LAG, 0), 0))
    return pl.pallas_call(
        functools.partial(_ffn_kernel, alpha, ff_chunk),
        grid=(n_tiles + FFN_PIPELINE_LAG,),
        in_specs=[tile_in, tile_in, _resident(w_out.shape), _resident(w1.shape),
                  _resident(w2.shape), _resident(g1.shape), _resident(b1.shape),
                  _resident(g2.shape), _resident(b2.shape)],
        out_specs=tile_out,
        out_shape=jax.ShapeDtypeStruct((n, d), F32),
        scratch_shapes=[pltpu.VMEM((2, tm, d), BF16),
                        pltpu.VMEM((2, tm, d), F32),
                        pltpu.VMEM((tm + 8, d), F32),
                        pltpu.VMEM((tm + 8, d), F32),
                        pltpu.VMEM((tm + 8, d), F32)],
        compiler_params=pltpu.CompilerParams(
            dimension_semantics=("arbitrary",), vmem_limit_bytes=V7X_VMEM_LIMIT_BYTES),
        name="ffn",
    )(y, x2, w_out, w1, w2, g1, b1, g2, b2)


def _rel_bias_blocks(rel_table):
    heads = rel_table.shape[0]
    n_diag = Q_BLOCK + BAND_BLOCK - 1
    dist = PAD + (Q_BLOCK - 1) - np.arange(n_diag)
    diag = rel_table[:, np.clip(dist, -MAX_REL, MAX_REL) + MAX_REL].astype(F32) * LOG2E
    ext = jnp.pad(diag, ((0, 0), (0, 1)))
    skew = jnp.tile(ext, (1, Q_BLOCK))[:, :Q_BLOCK * n_diag].reshape(heads, Q_BLOCK, n_diag)
    toeplitz = skew[:, :, Q_BLOCK - 1:Q_BLOCK - 1 + BAND_BLOCK]
    q_chunk = np.arange(Q_BLOCK)[:, None] // CHUNK
    k_chunk = np.arange(BAND_BLOCK)[None, :] // CHUNK
    in_band = (k_chunk >= q_chunk) & (k_chunk <= q_chunk + N_PREV_CHUNKS)
    return jnp.where(in_band[None], toeplitz, -jnp.inf)


def kernel(x, w_in, gmlp_ln_g, gmlp_ln_b, w_spatial, b_spatial, rel_bias,
           gate_g_gmlp, gate_g_attn, w_out, ln1_g, ln1_b, w_ff1, w_ff2, ln2_g, ln2_b):
    batch, seq, d_model = x.shape
    depth = w_in.shape[0]
    alpha = (2.0 * depth) ** 0.25
    tm = 512

    avg_half = 2 * V7X_LANES
    group_of = np.arange(avg_half) // GMLP_GROUP_DIM
    avg = jnp.asarray((group_of[:, None] == group_of[None, :]) / GMLP_GROUP_DIM, BF16)

    h = x.reshape(batch * seq, d_model)
    for l in range(depth):
        u, vn, q, k, v = _in_proj(
            h, w_in[l].astype(BF16),
            gmlp_ln_g[l].reshape(1, GMLP_WIDTH), gmlp_ln_b[l].reshape(1, GMLP_WIDTH), avg, tm)
        ws = w_spatial[l].astype(BF16).reshape(GMLP_GROUPS // 2, 2 * GMLP_CHUNK, GMLP_CHUNK)
        bs = jnp.repeat(jnp.transpose(b_spatial[l]), GMLP_GROUP_DIM, axis=1)
        y = _mixer(u, vn, q, k, v, ws, bs, _rel_bias_blocks(rel_bias[l]),
                   gate_g_gmlp[l].reshape(1, GMLP_WIDTH), gate_g_attn[l].reshape(1, ATT_WIDTH), seq)
        h = _ffn(y, h, w_out[l].astype(BF16), w_ff1[l].astype(BF16), w_ff2[l].astype(BF16),
                 ln1_g[l].reshape(1, d_model), ln1_b[l].reshape(1, d_model),
                 ln2_g[l].reshape(1, d_model), ln2_b[l].reshape(1, d_model), alpha, tm, 1024)
    return h.reshape(batch, seq, d_model)
```

```python
import functools

import jax
import jax.numpy as jnp
import numpy as np
from jax import lax
from jax.experimental import pallas as pl
from jax.experimental.pallas import tpu as pltpu

CHUNK = 64
N_PREV_CHUNKS = 8
BAND = (N_PREV_CHUNKS + 1) * CHUNK
PAD = N_PREV_CHUNKS * CHUNK
ATT_HEADS = 8
HEAD_DIM = 64
ATT_WIDTH = ATT_HEADS * HEAD_DIM
MAX_REL = 256
GMLP_GROUPS = 8
GMLP_GROUP_DIM = 64
GMLP_WIDTH = GMLP_GROUPS * GMLP_GROUP_DIM
GMLP_CHUNK = 128
LN_EPS = 1e-5
RMS_EPS = 1e-6
LOG2E = 1.4426950408889634

Q_BLOCK = 256
KEY_TILE = 256
BAND_BLOCK = PAD + Q_BLOCK
BLOCKS_PER_TRIP = 3

V7X_LANES = 128
V7X_VMEM_LIMIT_BYTES = 56 * 1024 * 1024

BF16 = jnp.bfloat16
F32 = jnp.float32


def _layer_norm_rows(t, g, b):
    mu = jnp.mean(t, axis=-1, keepdims=True)
    d = t - mu
    var = jnp.mean(d * d, axis=-1, keepdims=True)
    return d * lax.rsqrt(var + LN_EPS) * g + b


def _resident(shape):
    return pl.BlockSpec(shape, lambda *_: (0,) * len(shape), pipeline_mode=pl.Buffered(1))


IN_PROJ_ROWS = 1024

def _in_proj_kernel(x_ref, w_ref, lng_ref, lnb_ref, avg_ref, u_ref, vn_ref, q_ref, k_ref, v_ref):
    xb = x_ref[...].astype(BF16)

    def proj(col):
        return jnp.dot(xb, w_ref[:, col * GMLP_WIDTH:(col + 1) * GMLP_WIDTH],
                       preferred_element_type=F32)

    u_ref[...] = jax.nn.gelu(proj(0)).astype(BF16)

    gv = jax.nn.gelu(proj(1))
    avg = avg_ref[...]
    half = avg.shape[0]

    def group_mean(t):
        tb = t.astype(BF16)
        return jnp.concatenate(
            [jnp.dot(tb[:, i * half:(i + 1) * half], avg, preferred_element_type=F32)
             for i in range(GMLP_WIDTH // half)], axis=-1)

    d = gv - group_mean(gv)
    var = group_mean(d * d)
    vn_ref[...] = (d * lax.rsqrt(var + LN_EPS) * lng_ref[...] + lnb_ref[...]).astype(BF16)

    q_ref[...] = (proj(2) * (HEAD_DIM ** -0.5 * LOG2E)).astype(BF16)
    k_ref[...] = proj(3).astype(BF16)
    v_ref[...] = proj(4).astype(BF16)


def _in_proj(x2, w_in, ln_g, ln_b, avg, tm):
    n, d = x2.shape
    width = GMLP_WIDTH
    out = jax.ShapeDtypeStruct((n, width), BF16)
    tile = pl.BlockSpec((tm, width), lambda i: (i, 0))
    return pl.pallas_call(
        _in_proj_kernel,
        grid=(n // tm,),
        in_specs=[pl.BlockSpec((tm, d), lambda i: (i, 0)),
                  _resident(w_in.shape), _resident(ln_g.shape), _resident(ln_b.shape),
                  _resident(avg.shape)],
        out_specs=[tile] * 5,
        out_shape=[out] * 5,
        compiler_params=pltpu.CompilerParams(
            dimension_semantics=("parallel",), vmem_limit_bytes=V7X_VMEM_LIMIT_BYTES),
        name="in_proj",
    )(x2, w_in, ln_g, ln_b, avg)


def _mixer_kernel(u_ref, vn_ref, q_ref, k_ref, v_ref, ws_ref, bs_ref, bias_ref, ga_ref, gb_ref,
                  y_ref, wsm_ref, vx_ref, s_ref, m_ref, p_ref, o_ref):
    seq = u_ref.shape[0]
    n_pairs = ATT_WIDTH // V7X_LANES

    row = lax.broadcasted_iota(jnp.int32, (GMLP_CHUNK, GMLP_CHUNK), 0) // CHUNK
    col = lax.broadcasted_iota(jnp.int32, (GMLP_CHUNK, GMLP_CHUNK), 1) // CHUNK
    causal = jnp.concatenate([row >= col, row >= col], axis=0)
    first_group = lax.broadcasted_iota(jnp.int32, (GMLP_CHUNK, V7X_LANES), 1) < GMLP_GROUP_DIM
    for p in range(n_pairs):
        wsm_ref[p] = jnp.where(causal, ws_ref[p], jnp.zeros_like(ws_ref[p]))

    def gate_windows(row0):
        win = [pl.ds(row0 + i * GMLP_CHUNK, GMLP_CHUNK) for i in range(2)]
        parts = [[], []]
        for p in range(n_pairs):
            lanes = slice(p * V7X_LANES, (p + 1) * V7X_LANES)
            vals = jnp.concatenate([vn_ref[win[0], lanes], vn_ref[win[1], lanes]], axis=1)
            r = jnp.dot(wsm_ref[p], vals, preferred_element_type=F32)
            for i in range(2):
                cols = slice(i * V7X_LANES, (i + 1) * V7X_LANES)
                parts[i].append(jnp.where(first_group, r[:GMLP_CHUNK, cols], r[GMLP_CHUNK:, cols]))
        for i in range(2):
            vm = jnp.concatenate(parts[i], axis=-1) + bs_ref[...]
            ya = u_ref[win[i], :].astype(F32) * vm
            ms = jnp.mean(ya * ya, axis=-1, keepdims=True)
            y_ref[win[i], :GMLP_WIDTH] = (ya * lax.rsqrt(ms + RMS_EPS) * ga_ref[...]).astype(BF16)

    lane = lax.broadcasted_iota(jnp.int32, (Q_BLOCK, V7X_LANES), 1)
    head_lanes = [lane < HEAD_DIM, lane >= HEAD_DIM]

    def expand_values(rows):
        for h in range(ATT_HEADS):
            pair = slice((h // 2) * V7X_LANES, (h // 2 + 1) * V7X_LANES)
            vp = v_ref[rows, pair]
            vx_ref[rows, h * V7X_LANES:(h + 1) * V7X_LANES] = jnp.where(
                head_lanes[h % 2], vp, jnp.zeros_like(vp))

    def attn_block(qrow0, tiles):
        nk = len(tiles) * KEY_TILE
        koff = tiles[0] * KEY_TILE
        qrows = pl.ds(qrow0, Q_BLOCK)
        krow0 = qrow0 - PAD + koff
        if not isinstance(krow0, int):
            krow0 = pl.multiple_of(krow0, KEY_TILE)
        krows = pl.ds(krow0, nk)

        def scores(h):
            pair = slice((h // 2) * V7X_LANES, (h // 2 + 1) * V7X_LANES)
            qp = q_ref[qrows, pair]
            qh = jnp.where(head_lanes[h % 2], qp, jnp.zeros_like(qp))
            s = lax.dot_general(qh, k_ref[krows, pair], (((1,), (1,)), ((), ())),
                                preferred_element_type=F32)
            s_ref[h % 2, :, :nk] = s + bias_ref[h, :, koff:koff + nk]

        def probs(h):
            m = jnp.max(s_ref[h % 2, :, :nk], axis=-1, keepdims=True)
            m_ref[h % 2] = jnp.broadcast_to(m, (Q_BLOCK, V7X_LANES))
            for t in range(nk // V7X_LANES):
                cols = slice(t * V7X_LANES, (t + 1) * V7X_LANES)
                p_ref[h % 2, :, cols] = jnp.exp2(s_ref[h % 2, :, cols] - m_ref[h % 2]).astype(BF16)

        def weighted(h):
            vals = jnp.concatenate([vx_ref[krows, h * V7X_LANES:(h + 1) * V7X_LANES],
                                    jnp.ones((nk, V7X_LANES), BF16)], axis=1)
            o = jnp.dot(p_ref[h % 2, :, :nk], vals, preferred_element_type=F32)
            o_ref[:, h * V7X_LANES:(h + 1) * V7X_LANES] = o[:, :V7X_LANES] / o[:, V7X_LANES:]

        expand_values(qrows)
        gate_windows(qrow0)

        scores(0)
        for h in range(ATT_HEADS):
            if h + 1 < ATT_HEADS:
                scores(h + 1)
            probs(h)
            if h >= 1:
                weighted(h - 1)
        weighted(ATT_HEADS - 1)

        yb = jnp.concatenate(
            [o_ref[:, 2 * p * V7X_LANES:(2 * p + 1) * V7X_LANES]
             + o_ref[:, (2 * p + 1) * V7X_LANES:(2 * p + 2) * V7X_LANES] for p in range(n_pairs)],
            axis=-1)
        ms = jnp.mean(yb * yb, axis=-1, keepdims=True)
        y_ref[qrows, GMLP_WIDTH:] = (yb * lax.rsqrt(ms + RMS_EPS) * gb_ref[...]).astype(BF16)

    band_tiles = BAND_BLOCK // KEY_TILE
    first_full = PAD // Q_BLOCK
    for b in range(first_full):
        attn_block(b * Q_BLOCK, tuple(range(band_tiles - 1 - b, band_tiles)))

    def full_block(b, carry):
        attn_block(pl.multiple_of(b * Q_BLOCK, Q_BLOCK), tuple(range(band_tiles)))
        return carry

    lax.fori_loop(first_full, seq // Q_BLOCK, full_block, 0, unroll=BLOCKS_PER_TRIP)


def _mixer(u, vn, q, k, v, ws, bs, bias, ga, gb, seq):
    n = u.shape[0]
    row = pl.BlockSpec((seq, GMLP_WIDTH), lambda b: (b, 0))
    return pl.pallas_call(
        _mixer_kernel,
        grid=(n // seq,),
        in_specs=[row] * 5 + [_resident(ws.shape), _resident(bs.shape), _resident(bias.shape),
                              _resident(ga.shape), _resident(gb.shape)],
        out_specs=pl.BlockSpec((seq, GMLP_WIDTH + ATT_WIDTH), lambda b: (b, 0)),
        out_shape=jax.ShapeDtypeStruct((n, GMLP_WIDTH + ATT_WIDTH), BF16),
        scratch_shapes=[pltpu.VMEM(ws.shape, BF16),
                        pltpu.VMEM((seq, ATT_HEADS * V7X_LANES), BF16),
                        pltpu.VMEM((2, Q_BLOCK, BAND_BLOCK), F32),
                        pltpu.VMEM((2, Q_BLOCK, V7X_LANES), F32),
                        pltpu.VMEM((2, Q_BLOCK, BAND_BLOCK), BF16),
                        pltpu.VMEM((Q_BLOCK, ATT_HEADS * V7X_LANES), F32)],
        compiler_params=pltpu.CompilerParams(
            dimension_semantics=("parallel",), vmem_limit_bytes=V7X_VMEM_LIMIT_BYTES),
        name="mixer",
    )(u, vn, q, k, v, ws, bs, bias, ga, gb)


FFN_PIPELINE_LAG = 2
FFN_ANCHORS_PER_CHUNK = 4
FFN_NORM_PIECE_ROWS = 16


def _ffn_kernel(alpha, ff_chunk, y_ref, x_ref, wout_ref, w1_ref, w2_ref,
                g1_ref, b1_ref, g2_ref, b2_ref, o_ref, hb_ref, res_ref, t_ref, z0_ref, z1_ref):
    i = pl.program_id(0)

    @pl.when(i == 0)
    def _():
        hb_ref[1] = jnp.zeros(hb_ref.shape[1:], BF16)
        res_ref[1] = jnp.zeros(res_ref.shape[1:], F32)
        z0_ref[...] = jnp.zeros(z0_ref.shape, F32)
        z1_ref[...] = jnp.zeros(z1_ref.shape, F32)

    def step(par):
        z_in, z_out = (z0_ref, z1_ref) if par == 0 else (z1_ref, z0_ref)
        tm = o_ref.shape[0]
        n_chunks = w1_ref.shape[1] // ff_chunk
        piece_index = 0
        t_ref[:tm, :] = jnp.dot(y_ref[...], wout_ref[...], preferred_element_type=F32)
        unknown_zero = jnp.minimum(i, 0)
        acc = res_ref[1 - par]
        for c in range(n_chunks):
            cols = slice(c * ff_chunk, (c + 1) * ff_chunk)
            a = jnp.dot(hb_ref[1 - par], w1_ref[:, cols], preferred_element_type=F32)
            def norm_pieces(fresh, piece_index, count):
                stride = fresh.shape[1] // FFN_ANCHORS_PER_CHUNK
                for j in range(FFN_ANCHORS_PER_CHUNK):
                    piece = fresh[:8, j * stride:j * stride + V7X_LANES]
                    t_ref[tm:tm + 8, :V7X_LANES] = piece
                    z_in[tm:tm + 8, :V7X_LANES] = piece
                    here = count // FFN_ANCHORS_PER_CHUNK + (j < count % FFN_ANCHORS_PER_CHUNK)
                    for _ in range(here):
                        first = piece_index * FFN_NORM_PIECE_ROWS
                        piece_index += 1
                        rows = pl.ds(pl.multiple_of(first + unknown_zero, 8),
                                     FFN_NORM_PIECE_ROWS)
                        static_rows = slice(first, first + FFN_NORM_PIECE_ROWS)
                        h = _layer_norm_rows(alpha * x_ref[static_rows, :] + t_ref[rows, :],
                                             g1_ref[...], b1_ref[...])
                        hb_ref[par, static_rows, :] = h.astype(BF16)
                        res_ref[par, static_rows, :] = alpha * h
                        o_ref[static_rows, :] = _layer_norm_rows(
                            z_in[rows, :], g2_ref[...], b2_ref[...])
                return piece_index

            groups_left = 2 * (n_chunks - c) - 1
            pieces_left = tm // FFN_NORM_PIECE_ROWS - piece_index
            piece_index = norm_pieces(a, piece_index, -(-pieces_left // groups_left))
            a = jnp.maximum(a, 0.0)
            acc = acc + jnp.dot((a * a).astype(BF16), w2_ref[cols, :],
                                preferred_element_type=F32)
            if groups_left > 1:
                pieces_left = tm // FFN_NORM_PIECE_ROWS - piece_index
                piece_index = norm_pieces(acc, piece_index, -(-pieces_left // (groups_left - 1)))
        z_out[:tm, :] = acc

    for par in range(2):
        pl.when(i % 2 == par)(functools.partial(step, par))


def _ffn(y, x2, w_out, w1, w2, g1, b1, g2, b2, alpha, tm, ff_chunk):
    n, d = x2.shape
    n_tiles = n // tm
    tile_in = pl.BlockSpec((tm, d), lambda i: (jnp.minimum(i, n_tiles - 1), 0))
    tile_out = pl.BlockSpec((tm, d), lambda i: (jnp.maximum(i - FFN_PIPELINE_LAG, 0), 0))
    return pl.pallas_call(
        functools.partial(_ffn_kernel, alpha, ff_chunk),
        grid=(n_tiles + FFN_PIPELINE_LAG,),
        in_specs=[tile_in, tile_in, _resident(w_out.shape), _resident(w1.shape),
                  _resident(w2.shape), _resident(g1.shape), _resident(b1.shape),
                  _resident(g2.shape), _resident(b2.shape)],
        out_specs=tile_out,
        out_shape=jax.ShapeDtypeStruct((n, d), F32),
        scratch_shapes=[pltpu.VMEM((2, tm, d), BF16),
                        pltpu.VMEM((2, tm, d), F32),
                        pltpu.VMEM((tm + 8, d), F32),
                        pltpu.VMEM((tm + 8, d), F32),
                        pltpu.VMEM((tm + 8, d), F32)],
        compiler_params=pltpu.CompilerParams(
            dimension_semantics=("arbitrary",), vmem_limit_bytes=V7X_VMEM_LIMIT_BYTES),
        name="ffn",
    )(y, x2, w_out, w1, w2, g1, b1, g2, b2)


def _rel_bias_blocks(rel_table):
    heads = rel_table.shape[0]
    n_diag = Q_BLOCK + BAND_BLOCK - 1
    dist = PAD + (Q_BLOCK - 1) - np.arange(n_diag)
    diag = rel_table[:, np.clip(dist, -MAX_REL, MAX_REL) + MAX_REL].astype(F32) * LOG2E
    ext = jnp.pad(diag, ((0, 0), (0, 1)))
    skew = jnp.tile(ext, (1, Q_BLOCK))[:, :Q_BLOCK * n_diag].reshape(heads, Q_BLOCK, n_diag)
    toeplitz = skew[:, :, Q_BLOCK - 1:Q_BLOCK - 1 + BAND_BLOCK]
    q_chunk = np.arange(Q_BLOCK)[:, None] // CHUNK
    k_chunk = np.arange(BAND_BLOCK)[None, :] // CHUNK
    in_band = (k_chunk >= q_chunk) & (k_chunk <= q_chunk + N_PREV_CHUNKS)
    return jnp.where(in_band[None], toeplitz, -jnp.inf)


def kernel(x, w_in, gmlp_ln_g, gmlp_ln_b, w_spatial, b_spatial, rel_bias,
           gate_g_gmlp, gate_g_attn, w_out, ln1_g, ln1_b, w_ff1, w_ff2, ln2_g, ln2_b):
    batch, seq, d_model = x.shape
    depth = w_in.shape[0]
    alpha = (2.0 * depth) ** 0.25
    tm = 512

    avg_half = 2 * V7X_LANES
    group_of = np.arange(avg_half) // GMLP_GROUP_DIM
    avg = jnp.asarray((group_of[:, None] == group_of[None, :]) / GMLP_GROUP_DIM, BF16)

    h = x.reshape(batch * seq, d_model)
    for l in range(depth):
        u, vn, q, k, v = _in_proj(
            h, w_in[l].astype(BF16),
            gmlp_ln_g[l].reshape(1, GMLP_WIDTH), gmlp_ln_b[l].reshape(1, GMLP_WIDTH), avg,
            IN_PROJ_ROWS)
        ws = w_spatial[l].astype(BF16).reshape(GMLP_GROUPS // 2, 2 * GMLP_CHUNK, GMLP_CHUNK)
        bs = jnp.repeat(jnp.transpose(b_spatial[l]), GMLP_GROUP_DIM, axis=1)
        y = _mixer(u, vn, q, k, v, ws, bs, _rel_bias_blocks(rel_bias[l]),
                   gate_g_gmlp[l].reshape(1, GMLP_WIDTH), gate_g_attn[l].reshape(1, ATT_WIDTH), seq)
        h = _ffn(y, h, w_out[l].astype(BF16), w_ff1[l].astype(BF16), w_ff2[l].astype(BF16),
                 ln1_g[l].reshape(1, d_model), ln1_b[l].reshape(1, d_model),
                 ln2_g[l].reshape(1, d_model), ln2_b[l].reshape(1, d_model), alpha, tm, 1024)
    return h.reshape(batch, seq, d_model)
```

```python
import functools

import jax
import jax.numpy as jnp
import numpy as np
from jax import lax
from jax.experimental import pallas as pl
from jax.experimental.pallas import tpu as pltpu

CHUNK = 64
N_PREV_CHUNKS = 8
BAND = (N_PREV_CHUNKS + 1) * CHUNK
PAD = N_PREV_CHUNKS * CHUNK
ATT_HEADS = 8
HEAD_DIM = 64
ATT_WIDTH = ATT_HEADS * HEAD_DIM
MAX_REL = 256
GMLP_GROUPS = 8
GMLP_GROUP_DIM = 64
GMLP_WIDTH = GMLP_GROUPS * GMLP_GROUP_DIM
GMLP_CHUNK = 128
LN_EPS = 1e-5
RMS_EPS = 1e-6
LOG2E = 1.4426950408889634

Q_BLOCK = 256
KEY_TILE = 256
BAND_BLOCK = PAD + Q_BLOCK
BLOCKS_PER_TRIP = 3

V7X_LANES = 128
V7X_VMEM_LIMIT_BYTES = 56 * 1024 * 1024

BF16 = jnp.bfloat16
F32 = jnp.float32


def _layer_norm_rows(t, g, b):
    mu = jnp.mean(t, axis=-1, keepdims=True)
    d = t - mu
    var = jnp.mean(d * d, axis=-1, keepdims=True)
    return d * lax.rsqrt(var + LN_EPS) * g + b


def _gelu_tanh(x):
    k = 2.0 * 0.7978845608028654 * LOG2E
    inner = x * (-k - (k * 0.044715) * (x * x))
    return x / (1.0 + jnp.exp2(inner))


def _resident(shape):
    return pl.BlockSpec(shape, lambda *_: (0,) * len(shape), pipeline_mode=pl.Buffered(1))


IN_PROJ_ROWS = 1024

def _in_proj_kernel(x_ref, w_ref, lng_ref, lnb_ref, avg_ref, u_ref, vn_ref, q_ref, k_ref, v_ref):
    xb = x_ref[...].astype(BF16)

    def proj(col):
        return jnp.dot(xb, w_ref[:, col * GMLP_WIDTH:(col + 1) * GMLP_WIDTH],
                       preferred_element_type=F32)

    u_ref[...] = _gelu_tanh(proj(0)).astype(BF16)

    gv = _gelu_tanh(proj(1))
    avg = avg_ref[...]
    half = avg.shape[0]

    def group_mean(t):
        tb = t.astype(BF16)
        return jnp.concatenate(
            [jnp.dot(tb[:, i * half:(i + 1) * half], avg, preferred_element_type=F32)
             for i in range(GMLP_WIDTH // half)], axis=-1)

    d = gv - group_mean(gv)
    var = group_mean(d * d)
    vn_ref[...] = (d * lax.rsqrt(var + LN_EPS) * lng_ref[...] + lnb_ref[...]).astype(BF16)

    q_ref[...] = (proj(2) * (HEAD_DIM ** -0.5 * LOG2E)).astype(BF16)
    k_ref[...] = proj(3).astype(BF16)
    v_ref[...] = proj(4).astype(BF16)


def _in_proj(x2, w_in, ln_g, ln_b, avg, tm):
    n, d = x2.shape
    width = GMLP_WIDTH
    out = jax.ShapeDtypeStruct((n, width), BF16)
    tile = pl.BlockSpec((tm, width), lambda i: (i, 0))
    return pl.pallas_call(
        _in_proj_kernel,
        grid=(n // tm,),
        in_specs=[pl.BlockSpec((tm, d), lambda i: (i, 0)),
                  _resident(w_in.shape), _resident(ln_g.shape), _resident(ln_b.shape),
                  _resident(avg.shape)],
        out_specs=[tile] * 5,
        out_shape=[out] * 5,
        compiler_params=pltpu.CompilerParams(
            dimension_semantics=("parallel",), vmem_limit_bytes=V7X_VMEM_LIMIT_BYTES),
        name="in_proj",
    )(x2, w_in, ln_g, ln_b, avg)


def _mixer_kernel(u_ref, vn_ref, q_ref, k_ref, v_ref, ws_ref, bs_ref, bias_ref, ga_ref, gb_ref,
                  y_ref, wsm_ref, vx_ref, s_ref, m_ref, p_ref, o_ref):
    seq = u_ref.shape[0]
    n_pairs = ATT_WIDTH // V7X_LANES

    row = lax.broadcasted_iota(jnp.int32, (GMLP_CHUNK, GMLP_CHUNK), 0) // CHUNK
    col = lax.broadcasted_iota(jnp.int32, (GMLP_CHUNK, GMLP_CHUNK), 1) // CHUNK
    causal = jnp.concatenate([row >= col, row >= col], axis=0)
    first_group = lax.broadcasted_iota(jnp.int32, (GMLP_CHUNK, V7X_LANES), 1) < GMLP_GROUP_DIM
    for p in range(n_pairs):
        wsm_ref[p] = jnp.where(causal, ws_ref[p], jnp.zeros_like(ws_ref[p]))

    def gate_windows(row0):
        win = [pl.ds(row0 + i * GMLP_CHUNK, GMLP_CHUNK) for i in range(2)]
        parts = [[], []]
        for p in range(n_pairs):
            lanes = slice(p * V7X_LANES, (p + 1) * V7X_LANES)
            vals = jnp.concatenate([vn_ref[win[0], lanes], vn_ref[win[1], lanes]], axis=1)
            r = jnp.dot(wsm_ref[p], vals, preferred_element_type=F32)
            for i in range(2):
                cols = slice(i * V7X_LANES, (i + 1) * V7X_LANES)
                parts[i].append(jnp.where(first_group, r[:GMLP_CHUNK, cols], r[GMLP_CHUNK:, cols]))
        for i in range(2):
            vm = jnp.concatenate(parts[i], axis=-1) + bs_ref[...]
            ya = u_ref[win[i], :].astype(F32) * vm
            ms = jnp.mean(ya * ya, axis=-1, keepdims=True)
            y_ref[win[i], :GMLP_WIDTH] = (ya * lax.rsqrt(ms + RMS_EPS) * ga_ref[...]).astype(BF16)

    lane = lax.broadcasted_iota(jnp.int32, (Q_BLOCK, V7X_LANES), 1)
    head_lanes = [lane < HEAD_DIM, lane >= HEAD_DIM]

    def expand_values(rows):
        for h in range(ATT_HEADS):
            pair = slice((h // 2) * V7X_LANES, (h // 2 + 1) * V7X_LANES)
            vp = v_ref[rows, pair]
            vx_ref[rows, h * V7X_LANES:(h + 1) * V7X_LANES] = jnp.where(
                head_lanes[h % 2], vp, jnp.zeros_like(vp))

    def attn_block(qrow0, tiles):
        nk = len(tiles) * KEY_TILE
        koff = tiles[0] * KEY_TILE
        qrows = pl.ds(qrow0, Q_BLOCK)
        krow0 = qrow0 - PAD + koff
        if not isinstance(krow0, int):
            krow0 = pl.multiple_of(krow0, KEY_TILE)
        krows = pl.ds(krow0, nk)

        def scores(h):
            pair = slice((h // 2) * V7X_LANES, (h // 2 + 1) * V7X_LANES)
            qp = q_ref[qrows, pair]
            qh = jnp.where(head_lanes[h % 2], qp, jnp.zeros_like(qp))
            s = lax.dot_general(qh, k_ref[krows, pair], (((1,), (1,)), ((), ())),
                                preferred_element_type=F32)
            s_ref[h % 2, :, :nk] = s + bias_ref[h, :, koff:koff + nk]

        def probs(h):
            m = jnp.max(s_ref[h % 2, :, :nk], axis=-1, keepdims=True)
            m_ref[h % 2] = jnp.broadcast_to(m, (Q_BLOCK, V7X_LANES))
            for t in range(nk // V7X_LANES):
                cols = slice(t * V7X_LANES, (t + 1) * V7X_LANES)
                p_ref[h % 2, :, cols] = jnp.exp2(s_ref[h % 2, :, cols] - m_ref[h % 2]).astype(BF16)

        def weighted(h):
            vals = jnp.concatenate([vx_ref[krows, h * V7X_LANES:(h + 1) * V7X_LANES],
                                    jnp.ones((nk, V7X_LANES), BF16)], axis=1)
            o = jnp.dot(p_ref[h % 2, :, :nk], vals, preferred_element_type=F32)
            o_ref[:, h * V7X_LANES:(h + 1) * V7X_LANES] = o[:, :V7X_LANES] / o[:, V7X_LANES:]

        expand_values(qrows)
        gate_windows(qrow0)

        scores(0)
        for h in range(ATT_HEADS):
            if h + 1 < ATT_HEADS:
                scores(h + 1)
            probs(h)
            if h >= 1:
                weighted(h - 1)
        weighted(ATT_HEADS - 1)

        yb = jnp.concatenate(
            [o_ref[:, 2 * p * V7X_LANES:(2 * p + 1) * V7X_LANES]
             + o_ref[:, (2 * p + 1) * V7X_LANES:(2 * p + 2) * V7X_LANES] for p in range(n_pairs)],
            axis=-1)
        ms = jnp.mean(yb * yb, axis=-1, keepdims=True)
        y_ref[qrows, GMLP_WIDTH:] = (yb * lax.rsqrt(ms + RMS_EPS) * gb_ref[...]).astype(BF16)

    band_tiles = BAND_BLOCK // KEY_TILE
    first_full = PAD // Q_BLOCK
    for b in range(first_full):
        attn_block(b * Q_BLOCK, tuple(range(band_tiles - 1 - b, band_tiles)))

    def full_block(b, carry):
        attn_block(pl.multiple_of(b * Q_BLOCK, Q_BLOCK), tuple(range(band_tiles)))
        return carry

    lax.fori_loop(first_full, seq // Q_BLOCK, full_block, 0, unroll=BLOCKS_PER_TRIP)


def _mixer(u, vn, q, k, v, ws, bs, bias, ga, gb, seq):
    n = u.shape[0]
    row = pl.BlockSpec((seq, GMLP_WIDTH), lambda b: (b, 0))
    return pl.pallas_call(
        _mixer_kernel,
        grid=(n // seq,),
        in_specs=[row] * 5 + [_resident(ws.shape), _resident(bs.shape), _resident(bias.shape),
                              _resident(ga.shape), _resident(gb.shape)],
        out_specs=pl.BlockSpec((seq, GMLP_WIDTH + ATT_WIDTH), lambda b: (b, 0)),
        out_shape=jax.ShapeDtypeStruct((n, GMLP_WIDTH + ATT_WIDTH), BF16),
        scratch_shapes=[pltpu.VMEM(ws.shape, BF16),
                        pltpu.VMEM((seq, ATT_HEADS * V7X_LANES), BF16),
                        pltpu.VMEM((2, Q_BLOCK, BAND_BLOCK), F32),
                        pltpu.VMEM((2, Q_BLOCK, V7X_LANES), F32),
                        pltpu.VMEM((2, Q_BLOCK, BAND_BLOCK), BF16),
                        pltpu.VMEM((Q_BLOCK, ATT_HEADS * V7X_LANES), F32)],
        compiler_params=pltpu.CompilerParams(
            dimension_semantics=("parallel",), vmem_limit_bytes=V7X_VMEM_LIMIT_BYTES),
        name="mixer",
    )(u, vn, q, k, v, ws, bs, bias, ga, gb)


FFN_PIPELINE_LAG = 2
FFN_ANCHORS_PER_CHUNK = 4
FFN_NORM_PIECE_ROWS = 16


def _ffn_kernel(alpha, ff_chunk, y_ref, x_ref, wout_ref, w1_ref, w2_ref,
                g1_ref, b1_ref, g2_ref, b2_ref, o_ref, hb_ref, res_ref, t_ref, z0_ref, z1_ref):
    i = pl.program_id(0)

    @pl.when(i == 0)
    def _():
        hb_ref[1] = jnp.zeros(hb_ref.shape[1:], BF16)
        res_ref[1] = jnp.zeros(res_ref.shape[1:], F32)
        z0_ref[...] = jnp.zeros(z0_ref.shape, F32)
        z1_ref[...] = jnp.zeros(z1_ref.shape, F32)

    def step(par):
        z_in, z_out = (z0_ref, z1_ref) if par == 0 else (z1_ref, z0_ref)
        tm = o_ref.shape[0]
        n_chunks = w1_ref.shape[1] // ff_chunk
        piece_index = 0
        t_ref[:tm, :] = jnp.dot(y_ref[...], wout_ref[...], preferred_element_type=F32)
        unknown_zero = jnp.minimum(i, 0)
        acc = res_ref[1 - par]
        for c in range(n_chunks):
            cols = slice(c * ff_chunk, (c + 1) * ff_chunk)
            a = jnp.dot(hb_ref[1 - par], w1_ref[:, cols], preferred_element_type=F32)
            def norm_pieces(fresh, piece_index, count):
                stride = fresh.shape[1] // FFN_ANCHORS_PER_CHUNK
                for j in range(FFN_ANCHORS_PER_CHUNK):
                    piece = fresh[:8, j * stride:j * stride + V7X_LANES]
                    t_ref[tm:tm + 8, :V7X_LANES] = piece
                    z_in[tm:tm + 8, :V7X_LANES] = piece
                    here = count // FFN_ANCHORS_PER_CHUNK + (j < count % FFN_ANCHORS_PER_CHUNK)
                    for _ in range(here):
                        first = piece_index * FFN_NORM_PIECE_ROWS
                        piece_index += 1
                        rows = pl.ds(pl.multiple_of(first + unknown_zero, 8),
                                     FFN_NORM_PIECE_ROWS)
                        static_rows = slice(first, first + FFN_NORM_PIECE_ROWS)
                        h = _layer_norm_rows(alpha * x_ref[static_rows, :] + t_ref[rows, :],
                                             g1_ref[...], b1_ref[...])
                        hb_ref[par, static_rows, :] = h.astype(BF16)
                        res_ref[par, static_rows, :] = alpha * h
                        o_ref[static_rows, :] = _layer_norm_rows(
                            z_in[rows, :], g2_ref[...], b2_ref[...])
                return piece_index

            groups_left = 2 * (n_chunks - c) - 1
            pieces_left = tm // FFN_NORM_PIECE_ROWS - piece_index
            piece_index = norm_pieces(a, piece_index, -(-pieces_left // groups_left))
            a = jnp.maximum(a, 0.0)
            acc = acc + jnp.dot((a * a).astype(BF16), w2_ref[cols, :],
                                preferred_element_type=F32)
            if groups_left > 1:
                pieces_left = tm // FFN_NORM_PIECE_ROWS - piece_index
                piece_index = norm_pieces(acc, piece_index, -(-pieces_left // (groups_left - 1)))
        z_out[:tm, :] = acc

    for par in range(2):
        pl.when(i % 2 == par)(functools.partial(step, par))


def _ffn(y, x2, w_out, w1, w2, g1, b1, g2, b2, alpha, tm, ff_chunk):
    n, d = x2.shape
    n_tiles = n // tm
    tile_in = pl.BlockSpec((tm, d), lambda i: (jnp.minimum(i, n_tiles - 1), 0))
    tile_out = pl.BlockSpec((tm, d), lambda i: (jnp.maximum(i - FFN_PIPELINE_LAG, 0), 0))
    return pl.pallas_call(
        functools.partial(_ffn_kernel, alpha, ff_chunk),
        grid=(n_tiles + FFN_PIPELINE_LAG,),
        in_specs=[tile_in, tile_in, _resident(w_out.shape), _resident(w1.shape),
                  _resident(w2.shape), _resident(g1.shape), _resident(b1.shape),
                  _resident(g2.shape), _resident(b2.shape)],
        out_specs=tile_out,
        out_shape=jax.ShapeDtypeStruct((n, d), F32),
        scratch_shapes=[pltpu.VMEM((2, tm, d), BF16),
                        pltpu.VMEM((2, tm, d), F32),
                        pltpu.VMEM((tm + 8, d), F32),
                        pltpu.VMEM((tm + 8, d), F32),
                        pltpu.VMEM((tm + 8, d), F32)],
        compiler_params=pltpu.CompilerParams(
            dimension_semantics=("arbitrary",), vmem_limit_bytes=V7X_VMEM_LIMIT_BYTES),
        name="ffn",
    )(y, x2, w_out, w1, w2, g1, b1, g2, b2)


def _rel_bias_blocks(rel_table):
    heads = rel_table.shape[0]
    n_off = N_PREV_CHUNKS + 1
    n_diag = 2 * CHUNK - 1
    rev = (n_diag - 1) - np.arange(n_diag)
    dist = PAD - CHUNK * np.arange(n_off)[:, None] + (rev[None, :] - (CHUNK - 1))
    diag = rel_table[:, np.clip(dist, -MAX_REL, MAX_REL) + MAX_REL].astype(F32) * LOG2E
    ext = jnp.pad(diag, ((0, 0), (0, 0), (0, 1)))
    skew = jnp.tile(ext, (1, 1, CHUNK))[:, :, :CHUNK * n_diag].reshape(
        heads, n_off, CHUNK, n_diag)
    blocks = skew[..., CHUNK - 1:]
    off_band = jnp.full((heads, CHUNK, CHUNK), -jnp.inf, F32)
    rows = [jnp.concatenate([blocks[:, b - a] if 0 <= b - a < n_off else off_band
                             for b in range(BAND_BLOCK // CHUNK)], axis=2)
            for a in range(Q_BLOCK // CHUNK)]
    return jnp.concatenate(rows, axis=1)


def kernel(x, w_in, gmlp_ln_g, gmlp_ln_b, w_spatial, b_spatial, rel_bias,
           gate_g_gmlp, gate_g_attn, w_out, ln1_g, ln1_b, w_ff1, w_ff2, ln2_g, ln2_b):
    batch, seq, d_model = x.shape
    depth = w_in.shape[0]
    alpha = (2.0 * depth) ** 0.25
    tm = 512

    avg_half = 2 * V7X_LANES
    group_of = np.arange(avg_half) // GMLP_GROUP_DIM
    avg = jnp.asarray((group_of[:, None] == group_of[None, :]) / GMLP_GROUP_DIM, BF16)

    h = x.reshape(batch * seq, d_model)
    for l in range(depth):
        u, vn, q, k, v = _in_proj(
            h, w_in[l].astype(BF16),
            gmlp_ln_g[l].reshape(1, GMLP_WIDTH), gmlp_ln_b[l].reshape(1, GMLP_WIDTH), avg,
            IN_PROJ_ROWS)
        ws = w_spatial[l].astype(BF16).reshape(GMLP_GROUPS // 2, 2 * GMLP_CHUNK, GMLP_CHUNK)
        bs = jnp.repeat(jnp.transpose(b_spatial[l]), GMLP_GROUP_DIM, axis=1)
        y = _mixer(u, vn, q, k, v, ws, bs, _rel_bias_blocks(rel_bias[l]),
                   gate_g_gmlp[l].reshape(1, GMLP_WIDTH), gate_g_attn[l].reshape(1, ATT_WIDTH), seq)
        h = _ffn(y, h, w_out[l].astype(BF16), w_ff1[l].astype(BF16), w_ff2[l].astype(BF16),
                 ln1_g[l].reshape(1, d_model), ln1_b[l].reshape(1, d_model),
                 ln2_g[l].reshape(1, d_model), ln2_b[l].reshape(1, d_model), alpha, tm, 1024)
    return h.reshape(batch, seq, d_model)
```

```python
import functools

import jax
import jax.numpy as jnp
import numpy as np
from jax import lax
from jax.experimental import pallas as pl
from jax.experimental.pallas import tpu as pltpu

CHUNK = 64
N_PREV_CHUNKS = 8
BAND = (N_PREV_CHUNKS + 1) * CHUNK
PAD = N_PREV_CHUNKS * CHUNK
ATT_HEADS = 8
HEAD_DIM = 64
ATT_WIDTH = ATT_HEADS * HEAD_DIM
MAX_REL = 256
GMLP_GROUPS = 8
GMLP_GROUP_DIM = 64
GMLP_WIDTH = GMLP_GROUPS * GMLP_GROUP_DIM
GMLP_CHUNK = 128
LN_EPS = 1e-5
RMS_EPS = 1e-6
LOG2E = 1.4426950408889634

Q_BLOCK = 256
KEY_TILE = 256
BAND_BLOCK = PAD + Q_BLOCK
BLOCKS_PER_TRIP = 3

V7X_LANES = 128
V7X_VMEM_LIMIT_BYTES = 56 * 1024 * 1024

BF16 = jnp.bfloat16
F32 = jnp.float32


def _layer_norm_rows(t, g, b):
    mu = jnp.mean(t, axis=-1, keepdims=True)
    d = t - mu
    var = jnp.mean(d * d, axis=-1, keepdims=True)
    return d * lax.rsqrt(var + LN_EPS) * g + b


def _gelu_tanh(x):
    k = 2.0 * 0.7978845608028654 * LOG2E
    inner = x * (-k - (k * 0.044715) * (x * x))
    return x / (1.0 + jnp.exp2(inner))


def _resident(shape):
    return pl.BlockSpec(shape, lambda *_: (0,) * len(shape), pipeline_mode=pl.Buffered(1))


IN_PROJ_ROWS = 2048
IN_PROJ_ROW_PARTS = 4

def _in_proj_kernel(x_ref, w_ref, lng_ref, lnb_ref, avg_ref, u_ref, vn_ref, q_ref, k_ref, v_ref):
    xb = x_ref[...].astype(BF16)

    avg = avg_ref[...]
    half = avg.shape[0]
    part_rows = x_ref.shape[0] // IN_PROJ_ROW_PARTS

    def group_mean(t):
        tb = t.astype(BF16)
        return jnp.concatenate(
            [jnp.dot(tb[:, i * half:(i + 1) * half], avg, preferred_element_type=F32)
             for i in range(GMLP_WIDTH // half)], axis=-1)

    for part in range(IN_PROJ_ROW_PARTS):
        rows = slice(part * part_rows, (part + 1) * part_rows)
        xp = xb[rows]

        def proj(col):
            return jnp.dot(xp, w_ref[:, col * GMLP_WIDTH:(col + 1) * GMLP_WIDTH],
                           preferred_element_type=F32)

        u_ref[rows, :] = _gelu_tanh(proj(0)).astype(BF16)
        q_ref[rows, :] = (proj(2) * (HEAD_DIM ** -0.5 * LOG2E)).astype(BF16)
        gv = _gelu_tanh(proj(1))
        k_ref[rows, :] = proj(3).astype(BF16)
        d = gv - group_mean(gv)
        var = group_mean(d * d)
        vn_ref[rows, :] = (d * lax.rsqrt(var + LN_EPS) * lng_ref[...] + lnb_ref[...]).astype(BF16)
        v_ref[rows, :] = proj(4).astype(BF16)


def _in_proj(x2, w_in, ln_g, ln_b, avg, tm):
    n, d = x2.shape
    width = GMLP_WIDTH
    out = jax.ShapeDtypeStruct((n, width), BF16)
    tile = pl.BlockSpec((tm, width), lambda i: (i, 0))
    return pl.pallas_call(
        _in_proj_kernel,
        grid=(n // tm,),
        in_specs=[pl.BlockSpec((tm, d), lambda i: (i, 0)),
                  _resident(w_in.shape), _resident(ln_g.shape), _resident(ln_b.shape),
                  _resident(avg.shape)],
        out_specs=[tile] * 5,
        out_shape=[out] * 5,
        compiler_params=pltpu.CompilerParams(
            dimension_semantics=("parallel",), vmem_limit_bytes=V7X_VMEM_LIMIT_BYTES),
        name="in_proj",
    )(x2, w_in, ln_g, ln_b, avg)


def _mixer_kernel(u_ref, vn_ref, q_ref, k_ref, v_ref, ws_ref, bs_ref, bias_ref, ga_ref, gb_ref,
                  y_ref, wsm_ref, vx_ref, s_ref, m_ref, p_ref, o_ref):
    seq = u_ref.shape[0]
    n_pairs = ATT_WIDTH // V7X_LANES

    row = lax.broadcasted_iota(jnp.int32, (GMLP_CHUNK, GMLP_CHUNK), 0) // CHUNK
    col = lax.broadcasted_iota(jnp.int32, (GMLP_CHUNK, GMLP_CHUNK), 1) // CHUNK
    causal = jnp.concatenate([row >= col, row >= col], axis=0)
    first_group = lax.broadcasted_iota(jnp.int32, (GMLP_CHUNK, V7X_LANES), 1) < GMLP_GROUP_DIM
    for p in range(n_pairs):
        wsm_ref[p] = jnp.where(causal, ws_ref[p], jnp.zeros_like(ws_ref[p]))

    def gate_windows(row0):
        win = [pl.ds(row0 + i * GMLP_CHUNK, GMLP_CHUNK) for i in range(2)]
        parts = [[], []]
        for p in range(n_pairs):
            lanes = slice(p * V7X_LANES, (p + 1) * V7X_LANES)
            vals = jnp.concatenate([vn_ref[win[0], lanes], vn_ref[win[1], lanes]], axis=1)
            r = jnp.dot(wsm_ref[p], vals, preferred_element_type=F32)
            for i in range(2):
                cols = slice(i * V7X_LANES, (i + 1) * V7X_LANES)
                parts[i].append(jnp.where(first_group, r[:GMLP_CHUNK, cols], r[GMLP_CHUNK:, cols]))
        for i in range(2):
            vm = jnp.concatenate(parts[i], axis=-1) + bs_ref[...]
            ya = u_ref[win[i], :].astype(F32) * vm
            ms = jnp.mean(ya * ya, axis=-1, keepdims=True)
            y_ref[win[i], :GMLP_WIDTH] = (ya * lax.rsqrt(ms + RMS_EPS) * ga_ref[...]).astype(BF16)

    lane = lax.broadcasted_iota(jnp.int32, (Q_BLOCK, V7X_LANES), 1)
    head_lanes = [lane < HEAD_DIM, lane >= HEAD_DIM]

    def expand_values(rows):
        for h in range(ATT_HEADS):
            pair = slice((h // 2) * V7X_LANES, (h // 2 + 1) * V7X_LANES)
            vp = v_ref[rows, pair]
            vx_ref[rows, h * V7X_LANES:(h + 1) * V7X_LANES] = jnp.where(
                head_lanes[h % 2], vp, jnp.zeros_like(vp))

    def attn_block(qrow0, tiles):
        nk = len(tiles) * KEY_TILE
        koff = tiles[0] * KEY_TILE
        qrows = pl.ds(qrow0, Q_BLOCK)
        krow0 = qrow0 - PAD + koff
        if not isinstance(krow0, int):
            krow0 = pl.multiple_of(krow0, KEY_TILE)
        krows = pl.ds(krow0, nk)

        def scores(h):
            pair = slice((h // 2) * V7X_LANES, (h // 2 + 1) * V7X_LANES)
            qp = q_ref[qrows, pair]
            qh = jnp.where(head_lanes[h % 2], qp, jnp.zeros_like(qp))
            s = lax.dot_general(qh, k_ref[krows, pair], (((1,), (1,)), ((), ())),
                                preferred_element_type=F32)
            s_ref[h % 2, :, :nk] = s + bias_ref[h, :, koff:koff + nk]

        def probs(h):
            m = jnp.max(s_ref[h % 2, :, :nk], axis=-1, keepdims=True)
            m_ref[h % 2] = jnp.broadcast_to(m, (Q_BLOCK, V7X_LANES))
            for t in range(nk // V7X_LANES):
                cols = slice(t * V7X_LANES, (t + 1) * V7X_LANES)
                p_ref[h % 2, :, cols] = jnp.exp2(s_ref[h % 2, :, cols] - m_ref[h % 2]).astype(BF16)

        def weighted(h):
            vals = jnp.concatenate([vx_ref[krows, h * V7X_LANES:(h + 1) * V7X_LANES],
                                    jnp.ones((nk, V7X_LANES), BF16)], axis=1)
            o = jnp.dot(p_ref[h % 2, :, :nk], vals, preferred_element_type=F32)
            o_ref[:, h * V7X_LANES:(h + 1) * V7X_LANES] = o[:, :V7X_LANES] / o[:, V7X_LANES:]

        expand_values(qrows)
        gate_windows(qrow0)

        scores(0)
        for h in range(ATT_HEADS):
            if h + 1 < ATT_HEADS:
                scores(h + 1)
            probs(h)
            if h >= 1:
                weighted(h - 1)
        weighted(ATT_HEADS - 1)

        yb = jnp.concatenate(
            [o_ref[:, 2 * p * V7X_LANES:(2 * p + 1) * V7X_LANES]
             + o_ref[:, (2 * p + 1) * V7X_LANES:(2 * p + 2) * V7X_LANES] for p in range(n_pairs)],
            axis=-1)
        ms = jnp.mean(yb * yb, axis=-1, keepdims=True)
        y_ref[qrows, GMLP_WIDTH:] = (yb * lax.rsqrt(ms + RMS_EPS) * gb_ref[...]).astype(BF16)

    band_tiles = BAND_BLOCK // KEY_TILE
    first_full = PAD // Q_BLOCK
    for b in range(first_full):
        attn_block(b * Q_BLOCK, tuple(range(band_tiles - 1 - b, band_tiles)))

    def full_block(b, carry):
        attn_block(pl.multiple_of(b * Q_BLOCK, Q_BLOCK), tuple(range(band_tiles)))
        return carry

    lax.fori_loop(first_full, seq // Q_BLOCK, full_block, 0, unroll=BLOCKS_PER_TRIP)


def _mixer(u, vn, q, k, v, ws, bs, bias, ga, gb, seq):
    n = u.shape[0]
    row = pl.BlockSpec((seq, GMLP_WIDTH), lambda b: (b, 0))
    return pl.pallas_call(
        _mixer_kernel,
        grid=(n // seq,),
        in_specs=[row] * 5 + [_resident(ws.shape), _resident(bs.shape), _resident(bias.shape),
                              _resident(ga.shape), _resident(gb.shape)],
        out_specs=pl.BlockSpec((seq, GMLP_WIDTH + ATT_WIDTH), lambda b: (b, 0)),
        out_shape=jax.ShapeDtypeStruct((n, GMLP_WIDTH + ATT_WIDTH), BF16),
        scratch_shapes=[pltpu.VMEM(ws.shape, BF16),
                        pltpu.VMEM((seq, ATT_HEADS * V7X_LANES), BF16),
                        pltpu.VMEM((2, Q_BLOCK, BAND_BLOCK), F32),
                        pltpu.VMEM((2, Q_BLOCK, V7X_LANES), F32),
                        pltpu.VMEM((2, Q_BLOCK, BAND_BLOCK), BF16),
                        pltpu.VMEM((Q_BLOCK, ATT_HEADS * V7X_LANES), F32)],
        compiler_params=pltpu.CompilerParams(
            dimension_semantics=("parallel",), vmem_limit_bytes=V7X_VMEM_LIMIT_BYTES),
        name="mixer",
    )(u, vn, q, k, v, ws, bs, bias, ga, gb)


FFN_PIPELINE_LAG = 2
FFN_ANCHORS_PER_CHUNK = 4
FFN_NORM_PIECE_ROWS = 16


def _ffn_kernel(alpha, ff_chunk, y_ref, x_ref, wout_ref, w1_ref, w2_ref,
                g1_ref, b1_ref, g2_ref, b2_ref, o_ref, hb_ref, res_ref, t_ref, z0_ref, z1_ref):
    i = pl.program_id(0)

    @pl.when(i == 0)
    def _():
        hb_ref[1] = jnp.zeros(hb_ref.shape[1:], BF16)
        res_ref[1] = jnp.zeros(res_ref.shape[1:], F32)
        z0_ref[...] = jnp.zeros(z0_ref.shape, F32)
        z1_ref[...] = jnp.zeros(z1_ref.shape, F32)

    def step(par):
        z_in, z_out = (z0_ref, z1_ref) if par == 0 else (z1_ref, z0_ref)
        tm = o_ref.shape[0]
        n_chunks = w1_ref.shape[1] // ff_chunk
        piece_index = 0
        t_ref[:tm, :] = jnp.dot(y_ref[...], wout_ref[...], preferred_element_type=F32)
        unknown_zero = jnp.minimum(i, 0)
        acc = res_ref[1 - par]
        for c in range(n_chunks):
            cols = slice(c * ff_chunk, (c + 1) * ff_chunk)
            a = jnp.dot(hb_ref[1 - par], w1_ref[:, cols], preferred_element_type=F32)
            def norm_pieces(fresh, piece_index, count):
                stride = fresh.shape[1] // FFN_ANCHORS_PER_CHUNK
                for j in range(FFN_ANCHORS_PER_CHUNK):
                    piece = fresh[:8, j * stride:j * stride + V7X_LANES]
                    t_ref[tm:tm + 8, :V7X_LANES] = piece
                    z_in[tm:tm + 8, :V7X_LANES] = piece
                    here = count // FFN_ANCHORS_PER_CHUNK + (j < count % FFN_ANCHORS_PER_CHUNK)
                    for _ in range(here):
                        first = piece_index * FFN_NORM_PIECE_ROWS
                        piece_index += 1
                        rows = pl.ds(pl.multiple_of(first + unknown_zero, 8),
                                     FFN_NORM_PIECE_ROWS)
                        static_rows = slice(first, first + FFN_NORM_PIECE_ROWS)
                        h = _layer_norm_rows(alpha * x_ref[static_rows, :] + t_ref[rows, :],
                                             g1_ref[...], b1_ref[...])
                        hb_ref[par, static_rows, :] = h.astype(BF16)
                        res_ref[par, static_rows, :] = alpha * h
                        o_ref[static_rows, :] = _layer_norm_rows(
                            z_in[rows, :], g2_ref[...], b2_ref[...])
                return piece_index

            groups_left = 2 * (n_chunks - c) - 1
            pieces_left = tm // FFN_NORM_PIECE_ROWS - piece_index
            piece_index = norm_pieces(a, piece_index, -(-pieces_left // groups_left))
            a = jnp.maximum(a, 0.0)
            acc = acc + jnp.dot((a * a).astype(BF16), w2_ref[cols, :],
                                preferred_element_type=F32)
            if groups_left > 1:
                pieces_left = tm // FFN_NORM_PIECE_ROWS - piece_index
                piece_index = norm_pieces(acc, piece_index, -(-pieces_left // (groups_left - 1)))
        z_out[:tm, :] = acc

    for par in range(2):
        pl.when(i % 2 == par)(functools.partial(step, par))


def _ffn(y, x2, w_out, w1, w2, g1, b1, g2, b2, alpha, tm, ff_chunk):
    n, d = x2.shape
    n_tiles = n // tm
    tile_in = pl.BlockSpec((tm, d), lambda i: (jnp.minimum(i, n_tiles - 1), 0))
    tile_out = pl.BlockSpec((tm, d), lambda i: (jnp.maximum(i - FFN_PIPELINE_LAG, 0), 0))
    return pl.pallas_call(
        functools.partial(_ffn_kernel, alpha, ff_chunk),
        grid=(n_tiles + FFN_PIPELINE_LAG,),
        in_specs=[tile_in, tile_in, _resident(w_out.shape), _resident(w1.shape),
                  _resident(w2.shape), _resident(g1.shape), _resident(b1.shape),
                  _resident(g2.shape), _resident(b2.shape)],
        out_specs=tile_out,
        out_shape=jax.ShapeDtypeStruct((n, d), F32),
        scratch_shapes=[pltpu.VMEM((2, tm, d), BF16),
                        pltpu.VMEM((2, tm, d), F32),
                        pltpu.VMEM((tm + 8, d), F32),
                        pltpu.VMEM((tm + 8, d), F32),
                        pltpu.VMEM((tm + 8, d), F32)],
        compiler_params=pltpu.CompilerParams(
            dimension_semantics=("arbitrary",), vmem_limit_bytes=V7X_VMEM_LIMIT_BYTES),
        name="ffn",
    )(y, x2, w_out, w1, w2, g1, b1, g2, b2)


def _rel_bias_blocks(rel_table):
    heads = rel_table.shape[0]
    n_off = N_PREV_CHUNKS + 1
    n_diag = 2 * CHUNK - 1
    rev = (n_diag - 1) - np.arange(n_diag)
    dist = PAD - CHUNK * np.arange(n_off)[:, None] + (rev[None, :] - (CHUNK - 1))
    diag = rel_table[:, np.clip(dist, -MAX_REL, MAX_REL) + MAX_REL].astype(F32) * LOG2E
    ext = jnp.pad(diag, ((0, 0), (0, 0), (0, 1)))
    skew = jnp.tile(ext, (1, 1, CHUNK))[:, :, :CHUNK * n_diag].reshape(
        heads, n_off, CHUNK, n_diag)
    blocks = skew[..., CHUNK - 1:]
    off_band = jnp.full((heads, CHUNK, CHUNK), -jnp.inf, F32)
    rows = [jnp.concatenate([blocks[:, b - a] if 0 <= b - a < n_off else off_band
                             for b in range(BAND_BLOCK // CHUNK)], axis=2)
            for a in range(Q_BLOCK // CHUNK)]
    return jnp.concatenate(rows, axis=1)


def kernel(x, w_in, gmlp_ln_g, gmlp_ln_b, w_spatial, b_spatial, rel_bias,
           gate_g_gmlp, gate_g_attn, w_out, ln1_g, ln1_b, w_ff1, w_ff2, ln2_g, ln2_b):
    batch, seq, d_model = x.shape
    depth = w_in.shape[0]
    alpha = (2.0 * depth) ** 0.25
    tm = 512

    avg_half = 2 * V7X_LANES
    group_of = np.arange(avg_half) // GMLP_GROUP_DIM
    avg = jnp.asarray((group_of[:, None] == group_of[None, :]) / GMLP_GROUP_DIM, BF16)

    h = x.reshape(batch * seq, d_model)
    for l in range(depth):
        u, vn, q, k, v = _in_proj(
            h, w_in[l].astype(BF16),
            gmlp_ln_g[l].reshape(1, GMLP_WIDTH), gmlp_ln_b[l].reshape(1, GMLP_WIDTH), avg,
            IN_PROJ_ROWS)
        ws = w_spatial[l].astype(BF16).reshape(GMLP_GROUPS // 2, 2 * GMLP_CHUNK, GMLP_CHUNK)
        bs = jnp.repeat(jnp.transpose(b_spatial[l]), GMLP_GROUP_DIM, axis=1)
        y = _mixer(u, vn, q, k, v, ws, bs, _rel_bias_blocks(rel_bias[l]),
                   gate_g_gmlp[l].reshape(1, GMLP_WIDTH), gate_g_attn[l].reshape(1, ATT_WIDTH), seq)
        h = _ffn(y, h, w_out[l].astype(BF16), w_ff1[l].astype(BF16), w_ff2[l].astype(BF16),
                 ln1_g[l].reshape(1, d_model), ln1_b[l].reshape(1, d_model),
                 ln2_g[l].reshape(1, d_model), ln2_b[l].reshape(1, d_model), alpha, tm, 1024)
    return h.reshape(batch, seq, d_model)
```

```python
import functools

import jax
import jax.numpy as jnp
import numpy as np
from jax import lax
from jax.experimental import pallas as pl
from jax.experimental.pallas import tpu as pltpu

CHUNK = 64
N_PREV_CHUNKS = 8
BAND = (N_PREV_CHUNKS + 1) * CHUNK
PAD = N_PREV_CHUNKS * CHUNK
ATT_HEADS = 8
HEAD_DIM = 64
ATT_WIDTH = ATT_HEADS * HEAD_DIM
MAX_REL = 256
GMLP_GROUPS = 8
GMLP_GROUP_DIM = 64
GMLP_WIDTH = GMLP_GROUPS * GMLP_GROUP_DIM
GMLP_CHUNK = 128
LN_EPS = 1e-5
RMS_EPS = 1e-6
LOG2E = 1.4426950408889634

Q_BLOCK = 256
KEY_TILE = 256
BAND_BLOCK = PAD + Q_BLOCK
BLOCKS_PER_TRIP = 3

V7X_LANES = 128
V7X_VMEM_LIMIT_BYTES = 56 * 1024 * 1024

BF16 = jnp.bfloat16
F32 = jnp.float32


def _layer_norm_rows(t, g, b):
    mu = jnp.mean(t, axis=-1, keepdims=True)
    d = t - mu
    var = jnp.mean(d * d, axis=-1, keepdims=True)
    return d * lax.rsqrt(var + LN_EPS) * g + b


def _gelu_tanh(x):
    k = 2.0 * 0.7978845608028654 * LOG2E
    inner = x * (-k - (k * 0.044715) * (x * x))
    return x / (1.0 + jnp.exp2(inner))


def _resident(shape):
    return pl.BlockSpec(shape, lambda *_: (0,) * len(shape), pipeline_mode=pl.Buffered(1))


IN_PROJ_ROWS = 2048
IN_PROJ_ROW_PARTS = 4

def _in_proj_kernel(x_ref, w_ref, lng_ref, lnb_ref, avg_ref, u_ref, vn_ref, q_ref, k_ref, v_ref):
    xb = x_ref[...].astype(BF16)

    avg = avg_ref[...]
    half = avg.shape[0]
    part_rows = x_ref.shape[0] // IN_PROJ_ROW_PARTS

    def group_mean(t):
        tb = t.astype(BF16)
        return jnp.concatenate(
            [jnp.dot(tb[:, i * half:(i + 1) * half], avg, preferred_element_type=F32)
             for i in range(GMLP_WIDTH // half)], axis=-1)

    for part in range(IN_PROJ_ROW_PARTS):
        rows = slice(part * part_rows, (part + 1) * part_rows)
        xp = xb[rows]

        def proj(col):
            return jnp.dot(xp, w_ref[:, col * GMLP_WIDTH:(col + 1) * GMLP_WIDTH],
                           preferred_element_type=F32)

        u_ref[rows, :] = _gelu_tanh(proj(0)).astype(BF16)
        q_ref[rows, :] = (proj(2) * (HEAD_DIM ** -0.5 * LOG2E)).astype(BF16)
        gv = _gelu_tanh(proj(1))
        k_ref[rows, :] = proj(3).astype(BF16)
        c = gv - group_mean(gv)
        d = c - group_mean(c)
        var = group_mean(d * d)
        vn_ref[rows, :] = (d * lax.rsqrt(var + LN_EPS) * lng_ref[...] + lnb_ref[...]).astype(BF16)
        v_ref[rows, :] = proj(4).astype(BF16)


def _in_proj(x2, w_in, ln_g, ln_b, avg, tm):
    n, d = x2.shape
    width = GMLP_WIDTH
    out = jax.ShapeDtypeStruct((n, width), BF16)
    tile = pl.BlockSpec((tm, width), lambda i: (i, 0))
    return pl.pallas_call(
        _in_proj_kernel,
        grid=(n // tm,),
        in_specs=[pl.BlockSpec((tm, d), lambda i: (i, 0)),
                  _resident(w_in.shape), _resident(ln_g.shape), _resident(ln_b.shape),
                  _resident(avg.shape)],
        out_specs=[tile] * 5,
        out_shape=[out] * 5,
        compiler_params=pltpu.CompilerParams(
            dimension_semantics=("parallel",), vmem_limit_bytes=V7X_VMEM_LIMIT_BYTES),
        name="in_proj",
    )(x2, w_in, ln_g, ln_b, avg)


def _mixer_kernel(u_ref, vn_ref, q_ref, k_ref, v_ref, ws_ref, bs_ref, bias_ref, ga_ref, gb_ref,
                  y_ref, wsm_ref, vx_ref, s_ref, m_ref, p_ref, o_ref):
    seq = u_ref.shape[0]
    n_pairs = ATT_WIDTH // V7X_LANES

    row = lax.broadcasted_iota(jnp.int32, (GMLP_CHUNK, GMLP_CHUNK), 0) // CHUNK
    col = lax.broadcasted_iota(jnp.int32, (GMLP_CHUNK, GMLP_CHUNK), 1) // CHUNK
    causal = jnp.concatenate([row >= col, row >= col], axis=0)
    first_group = lax.broadcasted_iota(jnp.int32, (GMLP_CHUNK, V7X_LANES), 1) < GMLP_GROUP_DIM
    for p in range(n_pairs):
        wsm_ref[p] = jnp.where(causal, ws_ref[p], jnp.zeros_like(ws_ref[p]))

    def gate_windows(row0):
        win = [pl.ds(row0 + i * GMLP_CHUNK, GMLP_CHUNK) for i in range(2)]
        parts = [[], []]
        for p in range(n_pairs):
            lanes = slice(p * V7X_LANES, (p + 1) * V7X_LANES)
            vals = jnp.concatenate([vn_ref[win[0], lanes], vn_ref[win[1], lanes]], axis=1)
            r = jnp.dot(wsm_ref[p], vals, preferred_element_type=F32)
            for i in range(2):
                cols = slice(i * V7X_LANES, (i + 1) * V7X_LANES)
                parts[i].append(jnp.where(first_group, r[:GMLP_CHUNK, cols], r[GMLP_CHUNK:, cols]))
        for i in range(2):
            vm = jnp.concatenate(parts[i], axis=-1) + bs_ref[...]
            ya = u_ref[win[i], :].astype(F32) * vm
            ms = jnp.mean(ya * ya, axis=-1, keepdims=True)
            y_ref[win[i], :GMLP_WIDTH] = (ya * lax.rsqrt(ms + RMS_EPS) * ga_ref[...]).astype(BF16)

    lane = lax.broadcasted_iota(jnp.int32, (Q_BLOCK, V7X_LANES), 1)
    head_lanes = [lane < HEAD_DIM, lane >= HEAD_DIM]

    def expand_values(rows):
        for h in range(ATT_HEADS):
            pair = slice((h // 2) * V7X_LANES, (h // 2 + 1) * V7X_LANES)
            vp = v_ref[rows, pair]
            vx_ref[rows, h * V7X_LANES:(h + 1) * V7X_LANES] = jnp.where(
                head_lanes[h % 2], vp, jnp.zeros_like(vp))

    def attn_block(qrow0, tiles):
        nk = len(tiles) * KEY_TILE
        koff = tiles[0] * KEY_TILE
        qrows = pl.ds(qrow0, Q_BLOCK)
        krow0 = qrow0 - PAD + koff
        if not isinstance(krow0, int):
            krow0 = pl.multiple_of(krow0, KEY_TILE)
        krows = pl.ds(krow0, nk)

        def scores(h):
            pair = slice((h // 2) * V7X_LANES, (h // 2 + 1) * V7X_LANES)
            qp = q_ref[qrows, pair]
            qh = jnp.where(head_lanes[h % 2], qp, jnp.zeros_like(qp))
            s = lax.dot_general(qh, k_ref[krows, pair], (((1,), (1,)), ((), ())),
                                preferred_element_type=F32)
            s_ref[h % 2, :, :nk] = s + bias_ref[h, :, koff:koff + nk]

        def probs(h):
            m = jnp.max(s_ref[h % 2, :, :nk], axis=-1, keepdims=True)
            m_ref[h % 2] = jnp.broadcast_to(m, (Q_BLOCK, V7X_LANES))
            for t in range(nk // V7X_LANES):
                cols = slice(t * V7X_LANES, (t + 1) * V7X_LANES)
                p_ref[h % 2, :, cols] = jnp.exp2(s_ref[h % 2, :, cols] - m_ref[h % 2]).astype(BF16)

        def weighted(h):
            vals = jnp.concatenate([vx_ref[krows, h * V7X_LANES:(h + 1) * V7X_LANES],
                                    jnp.ones((nk, V7X_LANES), BF16)], axis=1)
            o = jnp.dot(p_ref[h % 2, :, :nk], vals, preferred_element_type=F32)
            o_ref[:, h * V7X_LANES:(h + 1) * V7X_LANES] = o[:, :V7X_LANES] / o[:, V7X_LANES:]

        expand_values(qrows)
        gate_windows(qrow0)

        scores(0)
        for h in range(ATT_HEADS):
            if h + 1 < ATT_HEADS:
                scores(h + 1)
            probs(h)
            if h >= 1:
                weighted(h - 1)
        weighted(ATT_HEADS - 1)

        yb = jnp.concatenate(
            [o_ref[:, 2 * p * V7X_LANES:(2 * p + 1) * V7X_LANES]
             + o_ref[:, (2 * p + 1) * V7X_LANES:(2 * p + 2) * V7X_LANES] for p in range(n_pairs)],
            axis=-1)
        ms = jnp.mean(yb * yb, axis=-1, keepdims=True)
        y_ref[qrows, GMLP_WIDTH:] = (yb * lax.rsqrt(ms + RMS_EPS) * gb_ref[...]).astype(BF16)

    band_tiles = BAND_BLOCK // KEY_TILE
    first_full = PAD // Q_BLOCK
    for b in range(first_full):
        attn_block(b * Q_BLOCK, tuple(range(band_tiles - 1 - b, band_tiles)))

    def full_block(b, carry):
        attn_block(pl.multiple_of(b * Q_BLOCK, Q_BLOCK), tuple(range(band_tiles)))
        return carry

    lax.fori_loop(first_full, seq // Q_BLOCK, full_block, 0, unroll=BLOCKS_PER_TRIP)


def _mixer(u, vn, q, k, v, ws, bs, bias, ga, gb, seq):
    n = u.shape[0]
    row = pl.BlockSpec((seq, GMLP_WIDTH), lambda b: (b, 0))
    return pl.pallas_call(
        _mixer_kernel,
        grid=(n // seq,),
        in_specs=[row] * 5 + [_resident(ws.shape), _resident(bs.shape), _resident(bias.shape),
                              _resident(ga.shape), _resident(gb.shape)],
        out_specs=pl.BlockSpec((seq, GMLP_WIDTH + ATT_WIDTH), lambda b: (b, 0)),
        out_shape=jax.ShapeDtypeStruct((n, GMLP_WIDTH + ATT_WIDTH), BF16),
        scratch_shapes=[pltpu.VMEM(ws.shape, BF16),
                        pltpu.VMEM((seq, ATT_HEADS * V7X_LANES), BF16),
                        pltpu.VMEM((2, Q_BLOCK, BAND_BLOCK), F32),
                        pltpu.VMEM((2, Q_BLOCK, V7X_LANES), F32),
                        pltpu.VMEM((2, Q_BLOCK, BAND_BLOCK), BF16),
                        pltpu.VMEM((Q_BLOCK, ATT_HEADS * V7X_LANES), F32)],
        compiler_params=pltpu.CompilerParams(
            dimension_semantics=("parallel",), vmem_limit_bytes=V7X_VMEM_LIMIT_BYTES),
        name="mixer",
    )(u, vn, q, k, v, ws, bs, bias, ga, gb)


FFN_PIPELINE_LAG = 2
FFN_ANCHORS_PER_CHUNK = 4
FFN_NORM_PIECE_ROWS = 16


def _ffn_kernel(alpha, ff_chunk, y_ref, x_ref, wout_ref, w1_ref, w2_ref,
                g1_ref, b1_ref, g2_ref, b2_ref, o_ref, hb_ref, res_ref, t_ref, z0_ref, z1_ref):
    i = pl.program_id(0)

    @pl.when(i == 0)
    def _():
        hb_ref[1] = jnp.zeros(hb_ref.shape[1:], BF16)
        res_ref[1] = jnp.zeros(res_ref.shape[1:], F32)
        z0_ref[...] = jnp.zeros(z0_ref.shape, F32)
        z1_ref[...] = jnp.zeros(z1_ref.shape, F32)

    def step(par):
        z_in, z_out = (z0_ref, z1_ref) if par == 0 else (z1_ref, z0_ref)
        tm = o_ref.shape[0]
        n_chunks = w1_ref.shape[1] // ff_chunk
        piece_index = 0
        t_ref[:tm, :] = jnp.dot(y_ref[...], wout_ref[...], preferred_element_type=F32)
        unknown_zero = jnp.minimum(i, 0)
        acc = res_ref[1 - par]
        for c in range(n_chunks):
            cols = slice(c * ff_chunk, (c + 1) * ff_chunk)
            a = jnp.dot(hb_ref[1 - par], w1_ref[:, cols], preferred_element_type=F32)
            def norm_pieces(fresh, piece_index, count):
                stride = fresh.shape[1] // FFN_ANCHORS_PER_CHUNK
                for j in range(FFN_ANCHORS_PER_CHUNK):
                    piece = fresh[:8, j * stride:j * stride + V7X_LANES]
                    t_ref[tm:tm + 8, :V7X_LANES] = piece
                    z_in[tm:tm + 8, :V7X_LANES] = piece
                    here = count // FFN_ANCHORS_PER_CHUNK + (j < count % FFN_ANCHORS_PER_CHUNK)
                    for _ in range(here):
                        first = piece_index * FFN_NORM_PIECE_ROWS
                        piece_index += 1
                        rows = pl.ds(pl.multiple_of(first + unknown_zero, 8),
                                     FFN_NORM_PIECE_ROWS)
                        static_rows = slice(first, first + FFN_NORM_PIECE_ROWS)
                        h = _layer_norm_rows(alpha * x_ref[static_rows, :] + t_ref[rows, :],
                                             g1_ref[...], b1_ref[...])
                        hb_ref[par, static_rows, :] = h.astype(BF16)
                        res_ref[par, static_rows, :] = alpha * h
                        o_ref[static_rows, :] = _layer_norm_rows(
                            z_in[rows, :], g2_ref[...], b2_ref[...])
                return piece_index

            groups_left = 2 * (n_chunks - c) - 1
            pieces_left = tm // FFN_NORM_PIECE_ROWS - piece_index
            piece_index = norm_pieces(a, piece_index, -(-pieces_left // groups_left))
            a = jnp.maximum(a, 0.0)
            acc = acc + jnp.dot((a * a).astype(BF16), w2_ref[cols, :],
                                preferred_element_type=F32)
            if groups_left > 1:
                pieces_left = tm // FFN_NORM_PIECE_ROWS - piece_index
                piece_index = norm_pieces(acc, piece_index, -(-pieces_left // (groups_left - 1)))
        z_out[:tm, :] = acc

    for par in range(2):
        pl.when(i % 2 == par)(functools.partial(step, par))


def _ffn(y, x2, w_out, w1, w2, g1, b1, g2, b2, alpha, tm, ff_chunk):
    n, d = x2.shape
    n_tiles = n // tm
    tile_in = pl.BlockSpec((tm, d), lambda i: (jnp.minimum(i, n_tiles - 1), 0))
    tile_out = pl.BlockSpec((tm, d), lambda i: (jnp.maximum(i - FFN_PIPELINE_LAG, 0), 0))
    return pl.pallas_call(
        functools.partial(_ffn_kernel, alpha, ff_chunk),
        grid=(n_tiles + FFN_PIPELINE_LAG,),
        in_specs=[tile_in, tile_in, _resident(w_out.shape), _resident(w1.shape),
                  _resident(w2.shape), _resident(g1.shape), _resident(b1.shape),
                  _resident(g2.shape), _resident(b2.shape)],
        out_specs=tile_out,
        out_shape=jax.ShapeDtypeStruct((n, d), F32),
        scratch_shapes=[pltpu.VMEM((2, tm, d), BF16),
                        pltpu.VMEM((2, tm, d), F32),
                        pltpu.VMEM((tm + 8, d), F32),
                        pltpu.VMEM((tm + 8, d), F32),
                        pltpu.VMEM((tm + 8, d), F32)],
        compiler_params=pltpu.CompilerParams(
            dimension_semantics=("arbitrary",), vmem_limit_bytes=V7X_VMEM_LIMIT_BYTES),
        name="ffn",
    )(y, x2, w_out, w1, w2, g1, b1, g2, b2)


def _rel_bias_blocks(rel_table):
    heads = rel_table.shape[0]
    n_off = N_PREV_CHUNKS + 1
    n_diag = 2 * CHUNK - 1
    rev = (n_diag - 1) - np.arange(n_diag)
    dist = PAD - CHUNK * np.arange(n_off)[:, None] + (rev[None, :] - (CHUNK - 1))
    diag = rel_table[:, np.clip(dist, -MAX_REL, MAX_REL) + MAX_REL].astype(F32) * LOG2E
    ext = jnp.pad(diag, ((0, 0), (0, 0), (0, 1)))
    skew = jnp.tile(ext, (1, 1, CHUNK))[:, :, :CHUNK * n_diag].reshape(
        heads, n_off, CHUNK, n_diag)
    blocks = skew[..., CHUNK - 1:]
    off_band = jnp.full((heads, CHUNK, CHUNK), -jnp.inf, F32)
    rows = [jnp.concatenate([blocks[:, b - a] if 0 <= b - a < n_off else off_band
                             for b in range(BAND_BLOCK // CHUNK)], axis=2)
            for a in range(Q_BLOCK // CHUNK)]
    return jnp.concatenate(rows, axis=1)


def kernel(x, w_in, gmlp_ln_g, gmlp_ln_b, w_spatial, b_spatial, rel_bias,
           gate_g_gmlp, gate_g_attn, w_out, ln1_g, ln1_b, w_ff1, w_ff2, ln2_g, ln2_b):
    batch, seq, d_model = x.shape
    depth = w_in.shape[0]
    alpha = (2.0 * depth) ** 0.25
    tm = 512

    avg_half = 2 * V7X_LANES
    group_of = np.arange(avg_half) // GMLP_GROUP_DIM
    avg = jnp.asarray((group_of[:, None] == group_of[None, :]) / GMLP_GROUP_DIM, BF16)

    h = x.reshape(batch * seq, d_model)
    for l in range(depth):
        u, vn, q, k, v = _in_proj(
            h, w_in[l].astype(BF16),
            gmlp_ln_g[l].reshape(1, GMLP_WIDTH), gmlp_ln_b[l].reshape(1, GMLP_WIDTH), avg,
            IN_PROJ_ROWS)
        ws = w_spatial[l].astype(BF16).reshape(GMLP_GROUPS // 2, 2 * GMLP_CHUNK, GMLP_CHUNK)
        bs = jnp.repeat(jnp.transpose(b_spatial[l]), GMLP_GROUP_DIM, axis=1)
        y = _mixer(u, vn, q, k, v, ws, bs, _rel_bias_blocks(rel_bias[l]),
                   gate_g_gmlp[l].reshape(1, GMLP_WIDTH), gate_g_attn[l].reshape(1, ATT_WIDTH), seq)
        h = _ffn(y, h, w_out[l].astype(BF16), w_ff1[l].astype(BF16), w_ff2[l].astype(BF16),
                 ln1_g[l].reshape(1, d_model), ln1_b[l].reshape(1, d_model),
                 ln2_g[l].reshape(1, d_model), ln2_b[l].reshape(1, d_model), alpha, tm, 1024)
    return h.reshape(batch, seq, d_model)
```
